```python
import math
import jax, jax.numpy as jnp
from jax import lax
import numpy as np

D_MODEL = 2048
BATCH = 2
SEQ = 16384
DEPTH = 1

CHUNK = 64
EPS = 1e-6
NEG_INF = -1e30

D_MIX = D_MODEL
DIFF_WIDTH = D_MIX // 2
GMLP_WIDTH = D_MIX - DIFF_WIDTH

DIFF_HEADS = 8
DIFF_QK_DIM = DIFF_WIDTH // DIFF_HEADS // 2
DIFF_V_DIM = 2 * DIFF_QK_DIM
ROPE_DIMS = DIFF_QK_DIM // 4
ROPE_THETA = 500000.0
Q_BLOCK = 128

GMLP_GROUPS = 8
GMLP_GROUP_CH = GMLP_WIDTH // GMLP_GROUPS
GMLP_LEN = 128

Q_COLS = DIFF_HEADS * 2 * DIFF_QK_DIM
K_COLS = DIFF_HEADS * 2 * DIFF_QK_DIM
V_COLS = DIFF_HEADS * DIFF_V_DIM
U_COLS = GMLP_WIDTH
Z_COLS = GMLP_WIDTH
IN_COLS = Q_COLS + K_COLS + V_COLS + U_COLS + Z_COLS

PEER_HEADS = 8
PEER_N_KEYS = 128
PEER_N_EXPERTS = PEER_N_KEYS * PEER_N_KEYS
PEER_QUERY_DIM = 256
PEER_HALF = PEER_QUERY_DIM // 2
PEER_TOPK = 16
PEER_TOKEN_BLOCK = 128

kernel_name = "hymba_diffattn_gmlp_peer_block"


def rmsnorm(x, g):
    xf = x.astype(jnp.float32)
    y = xf * lax.rsqrt(jnp.mean(xf * xf, axis=-1, keepdims=True) + EPS)
    return (y * g.astype(jnp.float32)).astype(x.dtype)


def rope_tables(seq_len, dtype):
    pos = jnp.arange(seq_len, dtype=jnp.float32)
    inv_freq = ROPE_THETA ** (-jnp.arange(0, ROPE_DIMS, 2, dtype=jnp.float32) / ROPE_DIMS)
    ang = pos[:, None] * inv_freq[None, :]
    return jnp.cos(ang).astype(dtype), jnp.sin(ang).astype(dtype)


def apply_partial_rope(t, cos, sin):
    half = ROPE_DIMS // 2
    c = cos[None, :, None, None, :]
    s = sin[None, :, None, None, :]
    x1 = t[..., :half]
    x2 = t[..., half:ROPE_DIMS]
    return jnp.concatenate([x1 * c - x2 * s, x2 * c + x1 * s, t[..., ROPE_DIMS:]], axis=-1)


def diff_attention(q, k, v, lam, subln_g, lambda_init):
    B, S = q.shape[0], q.shape[1]
    nqb = S // Q_BLOCK
    qb = q.reshape(B, nqb, Q_BLOCK, DIFF_HEADS, 2, DIFF_QK_DIM).transpose(1, 0, 3, 4, 2, 5)
    kt = k.transpose(0, 2, 3, 1, 4)
    vt = v.transpose(0, 2, 1, 3)
    chunk_id = jnp.arange(S, dtype=jnp.int32) // CHUNK
    q_chunk = chunk_id.reshape(nqb, Q_BLOCK)
    scale = DIFF_QK_DIM ** -0.5

    def one_block(args):
        q_blk, q_ch = args
        s = jnp.einsum('bhmqd,bhmkd->bhmqk', q_blk, kt).astype(jnp.float32) * scale
        mask = chunk_id[None, :] <= q_ch[:, None]
        s = jnp.where(mask, s, NEG_INF)
        p = jax.nn.softmax(s, axis=-1)
        a = p[:, :, 0] - lam * p[:, :, 1]
        return jnp.einsum('bhqk,bhkd->bhqd', a.astype(vt.dtype), vt)

    o = lax.map(one_block, (qb, q_chunk))
    o = o.transpose(1, 0, 3, 2, 4).reshape(B, S, DIFF_HEADS, DIFF_V_DIM)
    o = rmsnorm(o, subln_g) * (1.0 - lambda_init)
    return o.reshape(B, S, DIFF_WIDTH)


def gmlp_spatial_gating(u, z, z_norm_g, w_s, b_s):
    B, S = u.shape[0], u.shape[1]
    u = jax.nn.gelu(u)
    z = jax.nn.gelu(z).reshape(B, S, GMLP_GROUPS, GMLP_GROUP_CH)
    z = rmsnorm(z, z_norm_g.reshape(GMLP_GROUPS, GMLP_GROUP_CH))
    z = z.reshape(B, S // GMLP_LEN, GMLP_LEN, GMLP_GROUPS, GMLP_GROUP_CH)
    pos_chunk = jnp.arange(GMLP_LEN, dtype=jnp.int32) // CHUNK
    mask = pos_chunk[:, None] >= pos_chunk[None, :]
    w = jnp.where(mask[None], w_s, jnp.zeros_like(w_s))
    sz = jnp.einsum('gpq,bnqgc->bnpgc', w, z) + b_s.T[None, None, :, :, None]
    return u * sz.reshape(B, S, GMLP_WIDTH)


def peer_ffn(x, w_q, sub_keys, u_tab, v_tab):
    B, S, D = x.shape
    xt = x.reshape(-1, PEER_TOKEN_BLOCK, D)

    def one_block(xb):
        T = xb.shape[0]
        q = (xb @ w_q).reshape(T, PEER_HEADS, 2, PEER_HALF)
        sc = jnp.einsum('thcd,hckd->thck', q, sub_keys)
        s1, i1 = lax.top_k(sc[:, :, 0], PEER_TOPK)
        s2, i2 = lax.top_k(sc[:, :, 1], PEER_TOPK)
        cand = (s1[..., :, None] + s2[..., None, :]).reshape(T, PEER_HEADS, PEER_TOPK * PEER_TOPK)
        best, ci = lax.top_k(cand, PEER_TOPK)
        e1 = jnp.take_along_axis(i1, ci // PEER_TOPK, axis=-1)
        e2 = jnp.take_along_axis(i2, ci % PEER_TOPK, axis=-1)
        idx = e1 * PEER_N_KEYS + e2
        g = jax.nn.softmax(best.astype(jnp.float32), axis=-1)
        u_sel = jnp.take(u_tab, idx, axis=0)
        act = jax.nn.gelu(jnp.einsum('td,thkd->thk', xb, u_sel))
        v_sel = jnp.take(v_tab, idx, axis=0)
        return jnp.einsum('thk,thkd->td', g.astype(act.dtype) * act, v_sel)

    out = lax.map(one_block, xt)
    return out.reshape(B, S, D)


def setup_inputs(seed: int = 0) -> dict:
    key = jax.random.key(seed)
    ks = jax.random.split(key, 20)
    f32 = jnp.float32
    nrm = lambda k, shape, s: jax.random.normal(k, shape, f32) * s
    gain = lambda k, shape: 1.0 + 0.02 * jax.random.normal(k, shape, f32)
    return {
        "x": jax.random.normal(ks[0], (BATCH, SEQ, D_MODEL), f32),
        "mix_norm_g": gain(ks[1], (DEPTH, D_MODEL)),
        "w_in": nrm(ks[2], (DEPTH, D_MODEL, IN_COLS), D_MODEL ** -0.5),
        "lambda_q1": nrm(ks[3], (DEPTH, DIFF_QK_DIM), 0.1),
        "lambda_k1": nrm(ks[4], (DEPTH, DIFF_QK_DIM), 0.1),
        "lambda_q2": nrm(ks[5], (DEPTH, DIFF_QK_DIM), 0.1),
        "lambda_k2": nrm(ks[6], (DEPTH, DIFF_QK_DIM), 0.1),
        "subln_g": gain(ks[7], (DEPTH, DIFF_V_DIM)),
        "gmlp_z_norm_g": gain(ks[8], (DEPTH, GMLP_WIDTH)),
        "gmlp_w_s": nrm(ks[9], (DEPTH, GMLP_GROUPS, GMLP_LEN, GMLP_LEN), GMLP_LEN ** -0.5),
        "gmlp_b_s": gain(ks[10], (DEPTH, GMLP_GROUPS, GMLP_LEN)),
        "gmlp_out_g": gain(ks[11], (DEPTH, GMLP_WIDTH)),
        "w_out": nrm(ks[12], (DEPTH, D_MIX, D_MODEL), D_MIX ** -0.5),
        "ffn_norm_g": gain(ks[13], (DEPTH, D_MODEL)),
        "peer_w_q": nrm(ks[14], (DEPTH, D_MODEL, PEER_HEADS * PEER_QUERY_DIM), D_MODEL ** -0.5),
        "peer_sub_keys": nrm(ks[15], (DEPTH, PEER_HEADS, 2, PEER_N_KEYS, PEER_HALF), PEER_HALF ** -0.5),
        "peer_u": nrm(ks[16], (DEPTH, PEER_N_EXPERTS, D_MODEL), D_MODEL ** -0.5),
        "peer_v": nrm(ks[17], (DEPTH, PEER_N_EXPERTS, D_MODEL), 0.3),
        "final_norm_g": gain(ks[18], (D_MODEL,)),
    }


def reference(x, mix_norm_g, w_in, lambda_q1, lambda_k1, lambda_q2, lambda_k2, subln_g,
              gmlp_z_norm_g, gmlp_w_s, gmlp_b_s, gmlp_out_g, w_out, ffn_norm_g,
              peer_w_q, peer_sub_keys, peer_u, peer_v, final_norm_g):
    B, S = x.shape[0], x.shape[1]
    cos, sin = rope_tables(S, x.dtype)
    splits = [Q_COLS, Q_COLS + K_COLS, Q_COLS + K_COLS + V_COLS, Q_COLS + K_COLS + V_COLS + U_COLS]
    for l in range(DEPTH):
        lambda_init = 0.8 - 0.6 * math.exp(-0.3 * l)
        hn = rmsnorm(x, mix_norm_g[l])
        proj = hn @ w_in[l]
        q, k, v, u_g, z_g = jnp.split(proj, splits, axis=-1)
        q = apply_partial_rope(q.reshape(B, S, DIFF_HEADS, 2, DIFF_QK_DIM), cos, sin)
        k = apply_partial_rope(k.reshape(B, S, DIFF_HEADS, 2, DIFF_QK_DIM), cos, sin)
        v = v.reshape(B, S, DIFF_HEADS, DIFF_V_DIM)
        lam = (jnp.exp(jnp.sum(lambda_q1[l].astype(jnp.float32) * lambda_k1[l].astype(jnp.float32)))
               - jnp.exp(jnp.sum(lambda_q2[l].astype(jnp.float32) * lambda_k2[l].astype(jnp.float32)))
               + lambda_init)
        attn_out = diff_attention(q, k, v, lam, subln_g[l], lambda_init)
        gmlp_out = rmsnorm(gmlp_spatial_gating(u_g, z_g, gmlp_z_norm_g[l], gmlp_w_s[l], gmlp_b_s[l]),
                           gmlp_out_g[l])
        mixed = jnp.concatenate([attn_out, gmlp_out], axis=-1) @ w_out[l]
        x = x + mixed
        x = x + peer_ffn(rmsnorm(x, ffn_norm_g[l]), peer_w_q[l], peer_sub_keys[l], peer_u[l], peer_v[l])
    return rmsnorm(x, final_norm_g)
```

```python
import functools
import math

import jax
import jax.numpy as jnp
from jax import lax
from jax.experimental import pallas as pl
from jax.experimental.pallas import tpu as pltpu

F32 = jnp.float32
MXU_DTYPE = jnp.bfloat16

EPS = 1e-6
NEG_INF = -1e30
CHUNK = 64
ROPE_THETA = 500000.0

HEADS = 8
QK_DIM = 64
V_DIM = 128
ROPE_DIMS = QK_DIM // 4
GROUPS = 8
GROUP_CH = 128
GMLP_LEN = 128
WIDTH = 1024

PEER_HEADS = 8
PEER_KEYS = 128
PEER_TOPK = 16

LANES = 128
VMEM_LIMIT = 56 * 1024 * 1024


def _params(*sem):
    return pltpu.CompilerParams(dimension_semantics=sem, vmem_limit_bytes=VMEM_LIMIT)


def _rms(x, g):
    return x * lax.rsqrt(jnp.mean(x * x, axis=-1, keepdims=True) + EPS) * g


def _inproj_kernel(x_ref, g_ref, w_ref, cos_ref, sa_ref, sb_ref, zg_ref, o_ref, hn_ref):
    j = pl.program_id(1)

    @pl.when(j == 0)
    def _():
        hn_ref[...] = _rms(x_ref[...], g_ref[...]).astype(hn_ref.dtype)

    acc = jnp.dot(hn_ref[...], w_ref[...], preferred_element_type=F32)

    @pl.when(j < 2)
    def _():
        scale = jnp.where(j == 0, QK_DIM ** -0.5, 1.0).astype(F32)
        c, sa, sb = cos_ref[...], sa_ref[...], sb_ref[...]
        for h in range(HEADS):
            t = acc[:, h * LANES:(h + 1) * LANES]
            r = t * c + pltpu.roll(t, LANES - 8, 1) * sa + pltpu.roll(t, 8, 1) * sb
            o_ref[:, h * LANES:(h + 1) * LANES] = (r * scale).astype(o_ref.dtype)

    @pl.when(j == 2)
    def _():
        o_ref[...] = acc.astype(o_ref.dtype)

    @pl.when(j == 3)
    def _():
        o_ref[...] = jax.nn.gelu(acc).astype(o_ref.dtype)

    @pl.when(j == 4)
    def _():
        z = jax.nn.gelu(acc)
        zg = zg_ref[...]
        for g in range(GROUPS):
            sl = slice(g * GROUP_CH, (g + 1) * GROUP_CH)
            o_ref[:, sl] = _rms(z[:, sl], zg[:, sl]).astype(o_ref.dtype)


def _inproj(x2, g, w, cos_t, sa_t, sb_t, zg, seq, tm):
    n, d = x2.shape
    nseq = seq // tm
    tab = pl.BlockSpec((tm, LANES), lambda i, j: (i % nseq, 0))
    return pl.pallas_call(
        _inproj_kernel,
        grid=(n // tm, 5),
        in_specs=[
            pl.BlockSpec((tm, d), lambda i, j: (i, 0)),
            pl.BlockSpec((1, d), lambda i, j: (0, 0)),
            pl.BlockSpec((d, WIDTH), lambda i, j: (0, j)),
            tab, tab, tab,
            pl.BlockSpec((1, WIDTH), lambda i, j: (0, 0)),
        ],
        out_specs=pl.BlockSpec((tm, WIDTH), lambda i, j: (i, j)),
        out_shape=jax.ShapeDtypeStruct((n, 5 * WIDTH), MXU_DTYPE),
        scratch_shapes=[pltpu.VMEM((tm, d), MXU_DTYPE)],
        compiler_params=_params("parallel", "arbitrary"),
        name="inproj",
    )(x2, g, w, cos_t, sa_t, sb_t, zg)


def _attn_kernel(lam_ref, q_ref, k_ref, v_ref, g_ref, o_ref, vt_ref, m_ref, l_ref, acc_ref,
                 *, tq, post_scale):
    qi = pl.program_id(2)
    nblk = vt_ref.shape[0]

    @pl.when(qi == 0)
    def _():
        def tr(c, carry):
            off = pl.multiple_of(c * tq, tq)
            vt_ref[c] = v_ref[pl.ds(off, tq), :].astype(F32).T.astype(vt_ref.dtype)
            return carry
        lax.fori_loop(0, nblk, tr, 0)

    qt = q_ref[...].astype(F32).T
    row = lax.broadcasted_iota(jnp.int32, qt.shape, 0)
    qpad = jnp.concatenate([jnp.where(row < QK_DIM, qt, 0.0),
                            jnp.where(row >= QK_DIM, qt, 0.0)], axis=1).astype(k_ref.dtype)

    m_ref[...] = jnp.full(m_ref.shape, NEG_INF, F32)
    l_ref[...] = jnp.zeros(l_ref.shape, F32)
    acc_ref[...] = jnp.zeros(acc_ref.shape, F32)

    def step(j, masked):
        off = pl.multiple_of(j * tq, tq)
        s = jnp.dot(k_ref[pl.ds(off, tq), :], qpad, preferred_element_type=F32)
        if masked:
            kc = lax.broadcasted_iota(jnp.int32, s.shape, 0) // CHUNK
            qc = (lax.broadcasted_iota(jnp.int32, s.shape, 1) % tq) // CHUNK
            s = jnp.where(kc <= qc, s, NEG_INF)
        m_old = m_ref[...]
        m_new = jnp.maximum(m_old, jnp.max(s, axis=0, keepdims=True))
        alpha = jnp.exp(m_old - m_new)
        p = jnp.exp(s - m_new)
        l_ref[...] = alpha * l_ref[...] + jnp.sum(p, axis=0, keepdims=True)
        acc_ref[...] = alpha * acc_ref[...] + jnp.dot(
            vt_ref[j], p.astype(vt_ref.dtype), preferred_element_type=F32)
        m_ref[...] = m_new

    def body(j, carry):
        step(j, False)
        return carry
    lax.fori_loop(0, qi, body, 0)
    step(qi, True)

    inv = 1.0 / l_ref[...]
    o = acc_ref[...] * inv
    ot = o[:, :tq] - lam_ref[0] * o[:, tq:]
    o_ref[...] = (_rms(ot.T, g_ref[...]) * post_scale).astype(o_ref.dtype)


def _attn(lam, proj3, subln_g, post_scale, tq):
    b, s, _ = proj3.shape
    kern = functools.partial(_attn_kernel, tq=tq, post_scale=post_scale)
    return pl.pallas_call(
        kern,
        grid=(b, HEADS, s // tq),
        in_specs=[
            pl.BlockSpec(memory_space=pltpu.SMEM),
            pl.BlockSpec((None, tq, LANES), lambda bi, h, qi: (bi, qi, h)),
            pl.BlockSpec((None, s, LANES), lambda bi, h, qi: (bi, 0, HEADS + h)),
            pl.BlockSpec((None, s, LANES), lambda bi, h, qi: (bi, 0, 2 * HEADS + h)),
            pl.BlockSpec((1, V_DIM), lambda bi, h, qi: (0, 0)),
        ],
        out_specs=pl.BlockSpec((None, tq, LANES), lambda bi, h, qi: (bi, qi, h)),
        out_shape=jax.ShapeDtypeStruct((b, s, WIDTH), MXU_DTYPE),
        scratch_shapes=[
            pltpu.VMEM((s // tq, V_DIM, tq), MXU_DTYPE),
            pltpu.VMEM((1, 2 * tq), F32),
            pltpu.VMEM((1, 2 * tq), F32),
            pltpu.VMEM((V_DIM, 2 * tq), F32),
        ],
        compiler_params=_params("parallel", "parallel", "arbitrary"),
        name="attn",
    )(lam, proj3, proj3, proj3, subln_g)


def _gmlp_kernel(u_ref, z_ref, w_ref, b_ref, g_ref, o_ref, gm_ref):
    tm = u_ref.shape[0]
    pc = lax.broadcasted_iota(jnp.int32, (GMLP_LEN, GMLP_LEN), 0) // CHUNK
    qc = lax.broadcasted_iota(jnp.int32, (GMLP_LEN, GMLP_LEN), 1) // CHUNK
    for g in range(GROUPS):
        w = jnp.where(pc >= qc, w_ref[g], 0.0).astype(z_ref.dtype)
        bias = b_ref[g]
        cs = slice(g * GROUP_CH, (g + 1) * GROUP_CH)
        for nb in range(tm // GMLP_LEN):
            rs = slice(nb * GMLP_LEN, (nb + 1) * GMLP_LEN)
            sz = jnp.dot(w, z_ref[rs, cs], preferred_element_type=F32) + bias
            gm_ref[rs, cs] = u_ref[rs, cs].astype(F32) * sz
    o_ref[...] = _rms(gm_ref[...], g_ref[...]).astype(o_ref.dtype)


def _gmlp(proj, w_s, b_full, out_g, tm):
    n = proj.shape[0]
    return pl.pallas_call(
        _gmlp_kernel,
        grid=(n // tm,),
        in_specs=[
            pl.BlockSpec((tm, WIDTH), lambda i: (i, 3)),
            pl.BlockSpec((tm, WIDTH), lambda i: (i, 4)),
            pl.BlockSpec((GROUPS, GMLP_LEN, GMLP_LEN), lambda i: (0, 0, 0)),
            pl.BlockSpec((GROUPS, GMLP_LEN, GROUP_CH), lambda i: (0, 0, 0)),
            pl.BlockSpec((1, WIDTH), lambda i: (0, 0)),
        ],
        out_specs=pl.BlockSpec((tm, WIDTH), lambda i: (i, 0)),
        out_shape=jax.ShapeDtypeStruct((n, WIDTH), MXU_DTYPE),
        scratch_shapes=[pltpu.VMEM((tm, WIDTH), F32)],
        compiler_params=_params("parallel"),
        name="gmlp",
    )(proj, proj, w_s, b_full, out_g)


def _outproj_kernel(a_ref, gm_ref, x_ref, w_ref, g_ref, x1_ref, xnt_ref):
    mixed = (jnp.dot(a_ref[...], w_ref[:WIDTH, :], preferred_element_type=F32)
             + jnp.dot(gm_ref[...], w_ref[WIDTH:, :], preferred_element_type=F32))
    x1 = x_ref[...] + mixed
    x1_ref[...] = x1
    xnt_ref[...] = _rms(x1, g_ref[...]).T.astype(xnt_ref.dtype)


def _outproj(attn2, gm, x2, w_out, ffn_g, tm):
    n, d = x2.shape
    return pl.pallas_call(
        _outproj_kernel,
        grid=(n // tm,),
        in_specs=[
            pl.BlockSpec((tm, WIDTH), lambda i: (i, 0)),
            pl.BlockSpec((tm, WIDTH), lambda i: (i, 0)),
            pl.BlockSpec((tm, d), lambda i: (i, 0)),
            pl.BlockSpec((2 * WIDTH, d), lambda i: (0, 0)),
            pl.BlockSpec((1, d), lambda i: (0, 0)),
        ],
        out_specs=[
            pl.BlockSpec((tm, d), lambda i: (i, 0)),
            pl.BlockSpec((d, tm), lambda i: (0, i)),
        ],
        out_shape=[
            jax.ShapeDtypeStruct((n, d), F32),
            jax.ShapeDtypeStruct((d, n), MXU_DTYPE),
        ],
        compiler_params=_params("parallel"),
        name="outproj",
    )(attn2, gm, x2, w_out, ffn_g)


def _peerq_kernel(w_ref, xnt_ref, o_ref):
    o_ref[...] = jnp.dot(w_ref[...], xnt_ref[...], preferred_element_type=F32).astype(o_ref.dtype)


def _peerq(w_qt, xnt, tn):
    dq, d = w_qt.shape
    n = xnt.shape[1]
    return pl.pallas_call(
        _peerq_kernel,
        grid=(n // tn,),
        in_specs=[
            pl.BlockSpec((dq, d), lambda i: (0, 0)),
            pl.BlockSpec((d, tn), lambda i: (0, i)),
        ],
        out_specs=pl.BlockSpec((dq, tn), lambda i: (0, i)),
        out_shape=jax.ShapeDtypeStruct((dq, n), MXU_DTYPE),
        compiler_params=_params("parallel"),
        name="peerq",
    )(w_qt, xnt)


def _top_desc(x, count):
    rows = []
    for _ in range(count):
        m = jnp.max(x, axis=0, keepdims=True)
        rows.append(m)
        x = jnp.where(x == m, -jnp.inf, x)
    return rows


def _peersel_kernel(q_ref, k_ref, s2_ref, e2_ref, thr_ref, cf_ref):
    nk = PEER_KEYS
    s1 = jnp.dot(k_ref[0], q_ref[:nk, :], preferred_element_type=F32)
    s2 = jnp.dot(k_ref[1], q_ref[nk:, :], preferred_element_type=F32)
    a1 = _top_desc(s1, PEER_TOPK + 1)
    a2 = _top_desc(s2, PEER_TOPK + 1)

    a2lo = jnp.concatenate(a2[:8], axis=0)
    sub = lax.broadcasted_iota(jnp.int32, a2lo.shape, 0)
    blocks = [a1[0] + a2lo, a1[0] + jnp.concatenate(a2[8:16], axis=0)]
    for r in range(1, 8):
        blocks.append(a1[r] + jnp.where(sub < (PEER_TOPK + 1) // (r + 1), a2lo, -jnp.inf))
    blocks.append(jnp.concatenate(a1[8:16], axis=0) + a2[0])
    tail = jnp.where(sub == 0, a1[0] + a2[PEER_TOPK],
                     jnp.where(sub == 1, a1[PEER_TOPK] + a2[0], -jnp.inf))
    blocks.append(tail)
    best = _top_desc(jnp.concatenate(blocks, axis=0), PEER_TOPK + 1)

    zsum = jnp.zeros_like(best[0])
    for kk in range(PEER_TOPK):
        zsum = zsum + jnp.exp(best[kk] - best[0])
    tau = 0.5 * (best[PEER_TOPK - 1] + best[PEER_TOPK])
    s2_ref[...] = s2
    e2_ref[...] = jnp.exp(s2 - a2[0])
    thr_ref[...] = tau - s1
    cf_ref[...] = jnp.exp(s1 - a1[0]) / zsum


def _peersel(qpt, keys, tn):
    n = qpt.shape[1]
    rows = PEER_HEADS * PEER_KEYS
    out = pl.BlockSpec((PEER_KEYS, tn), lambda i, h: (h, i))
    shape = jax.ShapeDtypeStruct((rows, n), F32)
    return pl.pallas_call(
        _peersel_kernel,
        grid=(n // tn, PEER_HEADS),
        in_specs=[
            pl.BlockSpec((2 * PEER_KEYS, tn), lambda i, h: (h, i)),
            pl.BlockSpec((None, 2, PEER_KEYS, PEER_KEYS), lambda i, h: (h, 0, 0, 0)),
        ],
        out_specs=[out, out, out, out],
        out_shape=[shape, shape, shape, shape],
        compiler_params=_params("parallel", "parallel"),
        name="peersel",
    )(qpt, keys)


def _peerffn_kernel(u_ref, vt_ref, xnt_ref, s2_ref, e2_ref, thr_ref, cf_ref, o_ref):
    e = pl.program_id(1)
    ec = u_ref.shape[0]
    nsub = ec // PEER_KEYS

    @pl.when(e == 0)
    def _():
        o_ref[...] = jnp.zeros(o_ref.shape, F32)

    act = jnp.dot(u_ref[...], xnt_ref[...], preferred_element_type=F32)
    parts = []
    for ii in range(nsub):
        i1 = e * nsub + ii
        coef = jnp.zeros((PEER_KEYS, act.shape[1]), F32)
        for h in range(PEER_HEADS):
            hs = slice(h * PEER_KEYS, (h + 1) * PEER_KEYS)
            thr = thr_ref[pl.ds(h * PEER_KEYS + i1, 1), :]
            cf = cf_ref[pl.ds(h * PEER_KEYS + i1, 1), :]
            coef = coef + jnp.where(s2_ref[hs, :] >= thr, e2_ref[hs, :] * cf, 0.0)
        a = act[ii * PEER_KEYS:(ii + 1) * PEER_KEYS, :]
        parts.append((coef * jax.nn.gelu(a)).astype(vt_ref.dtype))
    pt = jnp.concatenate(parts, axis=0)
    o_ref[...] += jnp.dot(vt_ref[...], pt, preferred_element_type=F32)


def _peerffn(u_tab, v_tab_t, xnt, s2, e2, thr, cf, tn, ec):
    ne, d = u_tab.shape
    n = xnt.shape[1]
    rows = PEER_HEADS * PEER_KEYS
    sel = pl.BlockSpec((rows, tn), lambda i, e: (0, i))
    return pl.pallas_call(
        _peerffn_kernel,
        grid=(n // tn, ne // ec),
        in_specs=[
            pl.BlockSpec((ec, d), lambda i, e: (e, 0)),
            pl.BlockSpec((d, ec), lambda i, e: (0, e)),
            pl.BlockSpec((d, tn), lambda i, e: (0, i)),
            sel, sel, sel, sel,
        ],
        out_specs=pl.BlockSpec((d, tn), lambda i, e: (0, i)),
        out_shape=jax.ShapeDtypeStruct((d, n), F32),
        compiler_params=_params("parallel", "arbitrary"),
        name="peerffn",
    )(u_tab, v_tab_t, xnt, s2, e2, thr, cf)


def _final_kernel(x1_ref, pt_ref, g_ref, o_ref):
    o_ref[...] = _rms(x1_ref[...] + pt_ref[...].T, g_ref[...])


def _final(x1, peer_t, g, tm):
    n, d = x1.shape
    return pl.pallas_call(
        _final_kernel,
        grid=(n // tm,),
        in_specs=[
            pl.BlockSpec((tm, d), lambda i: (i, 0)),
            pl.BlockSpec((d, tm), lambda i: (0, i)),
            pl.BlockSpec((1, d), lambda i: (0, 0)),
        ],
        out_specs=pl.BlockSpec((tm, d), lambda i: (i, 0)),
        out_shape=jax.ShapeDtypeStruct((n, d), F32),
        compiler_params=_params("parallel"),
        name="final",
    )(x1, peer_t, g)


def _rope_tables(seq):
    half = ROPE_DIMS // 2
    pos = jnp.arange(seq, dtype=F32)
    inv_freq = ROPE_THETA ** (-jnp.arange(0, ROPE_DIMS, 2, dtype=F32) / ROPE_DIMS)
    ang = pos[:, None] * inv_freq[None, :]
    cos, sin = jnp.cos(ang), jnp.sin(ang)
    ones = jnp.ones((seq, QK_DIM - ROPE_DIMS), F32)
    zeros = jnp.zeros((seq, half), F32)
    rest = jnp.zeros((seq, QK_DIM - ROPE_DIMS), F32)
    c = jnp.concatenate([cos, cos, ones], axis=1)
    sa = jnp.concatenate([-sin, zeros, rest], axis=1)
    sb = jnp.concatenate([zeros, sin, rest], axis=1)
    return tuple(jnp.concatenate([t, t], axis=1) for t in (c, sa, sb))


def _tile(n, want):
    t = min(n, want)
    assert n % t == 0, (n, t)
    return t


def kernel(x, mix_norm_g, w_in, lambda_q1, lambda_k1, lambda_q2, lambda_k2, subln_g,
           gmlp_z_norm_g, gmlp_w_s, gmlp_b_s, gmlp_out_g, w_out, ffn_norm_g,
           peer_w_q, peer_sub_keys, peer_u, peer_v, final_norm_g):
    b, s, d = x.shape
    n = b * s
    depth = w_in.shape[0]
    cos_t, sa_t, sb_t = _rope_tables(s)
    tm_in = _tile(s, 512)
    tq = _tile(s, 512)
    tm = _tile(n, 512)
    tn = _tile(n, 512)
    ec = 512

    x2 = x.reshape(n, d)
    for l in range(depth):
        lambda_init = 0.8 - 0.6 * math.exp(-0.3 * l)
        lam = (jnp.exp(jnp.sum(lambda_q1[l] * lambda_k1[l]))
               - jnp.exp(jnp.sum(lambda_q2[l] * lambda_k2[l])) + lambda_init).reshape(1).astype(F32)

        proj = _inproj(x2, mix_norm_g[l][None], w_in[l].astype(MXU_DTYPE), cos_t, sa_t, sb_t,
                       gmlp_z_norm_g[l][None], s, tm_in)
        attn = _attn(lam, proj.reshape(b, s, 5 * WIDTH), subln_g[l][None], 1.0 - lambda_init, tq)
        b_full = jnp.broadcast_to(gmlp_b_s[l][:, :, None], (GROUPS, GMLP_LEN, GROUP_CH))
        gm = _gmlp(proj, gmlp_w_s[l], b_full, gmlp_out_g[l][None], tm)
        x1, xnt = _outproj(attn.reshape(n, WIDTH), gm, x2, w_out[l].astype(MXU_DTYPE),
                           ffn_norm_g[l][None], tm)

        qpt = _peerq(peer_w_q[l].T.astype(MXU_DTYPE), xnt, tn)
        s2, e2, thr, cf = _peersel(qpt, peer_sub_keys[l].astype(MXU_DTYPE), tn)
        peer_t = _peerffn(peer_u[l].astype(MXU_DTYPE), peer_v[l].T.astype(MXU_DTYPE), xnt,
                          s2, e2, thr, cf, tn, ec)
        if l + 1 < depth:
            x2 = x1 + peer_t.T
    return _final(x1, peer_t, final_norm_g[None], tm).reshape(b, s, d)
```

```python
import functools
import math

import jax
import jax.numpy as jnp
from jax import lax
from jax.experimental import pallas as pl
from jax.experimental.pallas import tpu as pltpu

F32 = jnp.float32
MXU_DTYPE = jnp.bfloat16
GATE_DTYPE = jnp.bfloat16

EPS = 1e-6
NEG_INF = -1e30
CHUNK = 64
ROPE_THETA = 500000.0

HEADS = 8
QK_DIM = 64
V_DIM = 128
ROPE_DIMS = QK_DIM // 4
GROUPS = 8
GROUP_CH = 128
GMLP_LEN = 128
WIDTH = 1024

PEER_HEADS = 8
PEER_KEYS = 128
PEER_TOPK = 16

LANES = 128
MXU_COLS = 256
ATTN_COL_TILE = 2 * MXU_COLS
PEER_SUB = 2 * MXU_COLS
VMEM_LIMIT = 56 * 1024 * 1024


def _params(*sem):
    return pltpu.CompilerParams(dimension_semantics=sem, vmem_limit_bytes=VMEM_LIMIT)


def _rms(x, g):
    return x * lax.rsqrt(jnp.mean(x * x, axis=-1, keepdims=True) + EPS) * g


def _inproj_kernel(x_ref, g_ref, w_ref, cos_ref, sa_ref, sb_ref, zg_ref, o_ref, hn_ref):
    j = pl.program_id(1)

    @pl.when(j == 0)
    def _():
        hn_ref[...] = _rms(x_ref[...], g_ref[...]).astype(hn_ref.dtype)

    acc = jnp.dot(hn_ref[...], w_ref[...], preferred_element_type=F32)

    @pl.when(j < 2)
    def _():
        scale = jnp.where(j == 0, QK_DIM ** -0.5, 1.0).astype(F32)
        c, sa, sb = cos_ref[...], sa_ref[...], sb_ref[...]
        for h in range(HEADS):
            t = acc[:, h * LANES:(h + 1) * LANES]
            r = t * c + pltpu.roll(t, LANES - 8, 1) * sa + pltpu.roll(t, 8, 1) * sb
            o_ref[:, h * LANES:(h + 1) * LANES] = (r * scale).astype(o_ref.dtype)

    @pl.when(j == 2)
    def _():
        o_ref[...] = acc.astype(o_ref.dtype)

    @pl.when(j == 3)
    def _():
        o_ref[...] = jax.nn.gelu(acc).astype(o_ref.dtype)

    @pl.when(j == 4)
    def _():
        z = jax.nn.gelu(acc)
        zg = zg_ref[...]
        for g in range(GROUPS):
            sl = slice(g * GROUP_CH, (g + 1) * GROUP_CH)
            o_ref[:, sl] = _rms(z[:, sl], zg[:, sl]).astype(o_ref.dtype)


def _inproj(x2, g, w, cos_t, sa_t, sb_t, zg, seq, tm):
    n, d = x2.shape
    nseq = seq // tm
    tab = pl.BlockSpec((tm, LANES), lambda i, j: (i % nseq, 0))
    return pl.pallas_call(
        _inproj_kernel,
        grid=(n // tm, 5),
        in_specs=[
            pl.BlockSpec((tm, d), lambda i, j: (i, 0)),
            pl.BlockSpec((1, d), lambda i, j: (0, 0)),
            pl.BlockSpec((d, WIDTH), lambda i, j: (0, j)),
            tab, tab, tab,
            pl.BlockSpec((1, WIDTH), lambda i, j: (0, 0)),
        ],
        out_specs=pl.BlockSpec((tm, WIDTH), lambda i, j: (i, j)),
        out_shape=jax.ShapeDtypeStruct((n, 5 * WIDTH), MXU_DTYPE),
        scratch_shapes=[pltpu.VMEM((tm, d), MXU_DTYPE)],
        compiler_params=_params("parallel", "arbitrary"),
        name="inproj",
    )(x2, g, w, cos_t, sa_t, sb_t, zg)


def _attn_kernel(lam_ref, q_ref, k_ref, v_ref, g_ref, o_ref, vt_ref, qpad_ref, s_ref, m_ref, l_ref,
                 acc_ref, *, tq, post_scale):
    qi = pl.program_id(2)
    nblk = vt_ref.shape[0]
    ct = min(ATTN_COL_TILE, 2 * tq)
    ncol = 2 * tq // ct

    @pl.when(qi == 0)
    def _():
        def tr(c, carry):
            off = pl.multiple_of(c * tq, tq)
            vt_ref[c] = v_ref[pl.ds(off, tq), :].astype(F32).T.astype(vt_ref.dtype)
            return carry
        lax.fori_loop(0, nblk, tr, 0)

    qt = q_ref[...].astype(F32).T
    row = lax.broadcasted_iota(jnp.int32, qt.shape, 0)
    qpad_ref[...] = jnp.concatenate([jnp.where(row < QK_DIM, qt, 0.0),
                                     jnp.where(row >= QK_DIM, qt, 0.0)], axis=1).astype(qpad_ref.dtype)

    m_ref[...] = jnp.full(m_ref.shape, NEG_INF, F32)
    l_ref[...] = jnp.zeros(l_ref.shape, F32)
    acc_ref[...] = jnp.zeros(acc_ref.shape, F32)

    def scores(kb, c):
        cs = slice(c * ct, (c + 1) * ct)
        return jnp.dot(kb, qpad_ref[:, cs], preferred_element_type=F32)

    def consume(s, vb, c, masked):
        cs = slice(c * ct, (c + 1) * ct)
        if masked:
            kc = lax.broadcasted_iota(jnp.int32, s.shape, 0) // CHUNK
            qc = ((lax.broadcasted_iota(jnp.int32, s.shape, 1) + c * ct) % tq) // CHUNK
            s = jnp.where(kc <= qc, s, NEG_INF)
        m_old = m_ref[:, cs]
        m_new = jnp.maximum(m_old, jnp.max(s, axis=0, keepdims=True))
        alpha = jnp.exp(m_old - m_new)
        p = jnp.exp(s - m_new)
        l_ref[:, cs] = alpha * l_ref[:, cs] + jnp.sum(p, axis=0, keepdims=True)
        acc_ref[:, cs] = alpha * acc_ref[:, cs] + jnp.dot(
            vb, p.astype(vb.dtype), preferred_element_type=F32)
        m_ref[:, cs] = m_new

    kb0 = k_ref[pl.ds(0, tq), :]
    for c in range(ncol):
        s_ref[:, c * ct:(c + 1) * ct] = scores(kb0, c)

    def body(j, carry):
        off = pl.multiple_of((j + 1) * tq, tq)
        kb = k_ref[pl.ds(off, tq), :]
        vb = vt_ref[j]
        for c in range(ncol):
            cs = slice(c * ct, (c + 1) * ct)
            s_cur = s_ref[:, cs]
            s_next = scores(kb, c)
            consume(s_cur, vb, c, False)
            s_ref[:, cs] = s_next
        return carry
    lax.fori_loop(0, qi, body, 0)
    vb = vt_ref[qi]
    for c in range(ncol):
        consume(s_ref[:, c * ct:(c + 1) * ct], vb, c, True)

    inv = 1.0 / l_ref[...]
    o = acc_ref[...] * inv
    ot = o[:, :tq] - lam_ref[0] * o[:, tq:]
    o_ref[...] = (_rms(ot.T, g_ref[...]) * post_scale).astype(o_ref.dtype)


def _attn(lam, proj3, subln_g, post_scale, tq):
    b, s, _ = proj3.shape
    kern = functools.partial(_attn_kernel, tq=tq, post_scale=post_scale)
    return pl.pallas_call(
        kern,
        grid=(b, HEADS, s // tq),
        in_specs=[
            pl.BlockSpec(memory_space=pltpu.SMEM),
            pl.BlockSpec((None, tq, LANES), lambda bi, h, qi: (bi, qi, h)),
            pl.BlockSpec((None, s, LANES), lambda bi, h, qi: (bi, 0, HEADS + h)),
            pl.BlockSpec((None, s, LANES), lambda bi, h, qi: (bi, 0, 2 * HEADS + h)),
            pl.BlockSpec((1, V_DIM), lambda bi, h, qi: (0, 0)),
        ],
        out_specs=pl.BlockSpec((None, tq, LANES), lambda bi, h, qi: (bi, qi, h)),
        out_shape=jax.ShapeDtypeStruct((b, s, WIDTH), MXU_DTYPE),
        scratch_shapes=[
            pltpu.VMEM((s // tq, V_DIM, tq), MXU_DTYPE),
            pltpu.VMEM((2 * QK_DIM, 2 * tq), MXU_DTYPE),
            pltpu.VMEM((tq, 2 * tq), F32),
            pltpu.VMEM((1, 2 * tq), F32),
            pltpu.VMEM((1, 2 * tq), F32),
            pltpu.VMEM((V_DIM, 2 * tq), F32),
        ],
        compiler_params=_params("parallel", "parallel", "arbitrary"),
        name="attn",
    )(lam, proj3, proj3, proj3, subln_g)


def _gmlp_kernel(u_ref, z_ref, w_ref, b_ref, g_ref, o_ref, gm_ref):
    tm = u_ref.shape[0]
    pc = lax.broadcasted_iota(jnp.int32, (GMLP_LEN, GMLP_LEN), 0) // CHUNK
    qc = lax.broadcasted_iota(jnp.int32, (GMLP_LEN, GMLP_LEN), 1) // CHUNK
    for g in range(GROUPS):
        w = jnp.where(pc >= qc, w_ref[g], 0.0).astype(z_ref.dtype)
        bias = b_ref[g]
        cs = slice(g * GROUP_CH, (g + 1) * GROUP_CH)
        for nb in range(tm // GMLP_LEN):
            rs = slice(nb * GMLP_LEN, (nb + 1) * GMLP_LEN)
            sz = jnp.dot(w, z_ref[rs, cs], preferred_element_type=F32) + bias
            gm_ref[rs, cs] = u_ref[rs, cs].astype(F32) * sz
    o_ref[...] = _rms(gm_ref[...], g_ref[...]).astype(o_ref.dtype)


def _gmlp(proj, w_s, b_full, out_g, tm):
    n = proj.shape[0]
    return pl.pallas_call(
        _gmlp_kernel,
        grid=(n // tm,),
        in_specs=[
            pl.BlockSpec((tm, WIDTH), lambda i: (i, 3)),
            pl.BlockSpec((tm, WIDTH), lambda i: (i, 4)),
            pl.BlockSpec((GROUPS, GMLP_LEN, GMLP_LEN), lambda i: (0, 0, 0)),
            pl.BlockSpec((GROUPS, GMLP_LEN, GROUP_CH), lambda i: (0, 0, 0)),
            pl.BlockSpec((1, WIDTH), lambda i: (0, 0)),
        ],
        out_specs=pl.BlockSpec((tm, WIDTH), lambda i: (i, 0)),
        out_shape=jax.ShapeDtypeStruct((n, WIDTH), MXU_DTYPE),
        scratch_shapes=[pltpu.VMEM((tm, WIDTH), F32)],
        compiler_params=_params("parallel"),
        name="gmlp",
    )(proj, proj, w_s, b_full, out_g)


def _outproj_kernel(a_ref, gm_ref, x_ref, w_ref, g_ref, x1_ref, xnt_ref):
    mixed = (jnp.dot(a_ref[...], w_ref[:WIDTH, :], preferred_element_type=F32)
             + jnp.dot(gm_ref[...], w_ref[WIDTH:, :], preferred_element_type=F32))
    x1 = x_ref[...] + mixed
    x1_ref[...] = x1
    xnt_ref[...] = _rms(x1, g_ref[...]).T.astype(xnt_ref.dtype)


def _outproj(attn2, gm, x2, w_out, ffn_g, tm):
    n, d = x2.shape
    return pl.pallas_call(
        _outproj_kernel,
        grid=(n // tm,),
        in_specs=[
            pl.BlockSpec((tm, WIDTH), lambda i: (i, 0)),
            pl.BlockSpec((tm, WIDTH), lambda i: (i, 0)),
            pl.BlockSpec((tm, d), lambda i: (i, 0)),
            pl.BlockSpec((2 * WIDTH, d), lambda i: (0, 0)),
            pl.BlockSpec((1, d), lambda i: (0, 0)),
        ],
        out_specs=[
            pl.BlockSpec((tm, d), lambda i: (i, 0)),
            pl.BlockSpec((d, tm), lambda i: (0, i)),
        ],
        out_shape=[
            jax.ShapeDtypeStruct((n, d), F32),
            jax.ShapeDtypeStruct((d, n), MXU_DTYPE),
        ],
        compiler_params=_params("parallel"),
        name="outproj",
    )(attn2, gm, x2, w_out, ffn_g)


def _peerq_kernel(w_ref, xnt_ref, o_ref):
    o_ref[...] = jnp.dot(w_ref[...], xnt_ref[...], preferred_element_type=F32).astype(o_ref.dtype)


def _peerq(w_qt, xnt, tn):
    dq, d = w_qt.shape
    n = xnt.shape[1]
    return pl.pallas_call(
        _peerq_kernel,
        grid=(n // tn,),
        in_specs=[
            pl.BlockSpec((dq, d), lambda i: (0, 0)),
            pl.BlockSpec((d, tn), lambda i: (0, i)),
        ],
        out_specs=pl.BlockSpec((dq, tn), lambda i: (0, i)),
        out_shape=jax.ShapeDtypeStruct((dq, n), MXU_DTYPE),
        compiler_params=_params("parallel"),
        name="peerq",
    )(w_qt, xnt)


def _top_desc(x, count):
    rows = []
    rank = jnp.full(x.shape, PEER_KEYS - 1, F32)
    for r in range(count):
        m = jnp.max(x, axis=0, keepdims=True)
        rows.append(m)
        hit = x == m
        rank = jnp.where(hit, float(r), rank)
        x = jnp.where(hit, -jnp.inf, x)
    return rows, rank


def _peersel_kernel(q_ref, k_ref, r2_ref, e2_ref, n1_ref, cf_ref):
    nk = PEER_KEYS
    s1 = jnp.dot(k_ref[0], q_ref[:nk, :], preferred_element_type=F32)
    s2 = jnp.dot(k_ref[1], q_ref[nk:, :], preferred_element_type=F32)
    a1, _ = _top_desc(s1, PEER_TOPK + 1)
    a2, rank2 = _top_desc(s2, PEER_TOPK + 1)

    a2lo = jnp.concatenate(a2[:8], axis=0)
    sub = lax.broadcasted_iota(jnp.int32, a2lo.shape, 0)
    blocks = [a1[0] + a2lo, a1[0] + jnp.concatenate(a2[8:16], axis=0)]
    for r in range(1, 8):
        blocks.append(a1[r] + jnp.where(sub < (PEER_TOPK + 1) // (r + 1), a2lo, -jnp.inf))
    blocks.append(jnp.concatenate(a1[8:16], axis=0) + a2[0])
    tail = jnp.where(sub == 0, a1[0] + a2[PEER_TOPK],
                     jnp.where(sub == 1, a1[PEER_TOPK] + a2[0], -jnp.inf))
    blocks.append(tail)
    best, _ = _top_desc(jnp.concatenate(blocks, axis=0), PEER_TOPK + 1)

    zsum = jnp.zeros_like(best[0])
    for kk in range(PEER_TOPK):
        zsum = zsum + jnp.exp(best[kk] - best[0])
    thr = 0.5 * (best[PEER_TOPK - 1] + best[PEER_TOPK]) - s1
    n1 = jnp.zeros_like(s1)
    for c in range(PEER_TOPK + 1):
        n1 = n1 + jnp.where(a2[c] >= thr, 1.0, 0.0)
    r2_ref[...] = rank2.astype(r2_ref.dtype)
    e2_ref[...] = jnp.exp(s2 - a2[0]).astype(e2_ref.dtype)
    n1_ref[...] = n1
    cf_ref[...] = jnp.exp(s1 - a1[0]) / zsum


def _peersel(qpt, keys, tn):
    n = qpt.shape[1]
    rows = PEER_HEADS * PEER_KEYS
    out = pl.BlockSpec((PEER_KEYS, tn), lambda i, h: (h, i))
    shape = jax.ShapeDtypeStruct((rows, n), F32)
    gshape = jax.ShapeDtypeStruct((rows, n), GATE_DTYPE)
    return pl.pallas_call(
        _peersel_kernel,
        grid=(n // tn, PEER_HEADS),
        in_specs=[
            pl.BlockSpec((2 * PEER_KEYS, tn), lambda i, h: (h, i)),
            pl.BlockSpec((None, 2, PEER_KEYS, PEER_KEYS), lambda i, h: (h, 0, 0, 0)),
        ],
        out_specs=[out, out, out, out],
        out_shape=[gshape, gshape, shape, shape],
        compiler_params=_params("parallel", "parallel"),
        name="peersel",
    )(qpt, keys)


def _peerffn_kernel(u_ref, vt_ref, xnt_ref, r2_ref, e2_ref, n1_ref, cf_ref, o_ref):
    e = pl.program_id(1)
    ec = u_ref.shape[0]
    nsub = ec // PEER_SUB
    per = PEER_SUB // PEER_KEYS
    tn = xnt_ref.shape[1]
    gdt = r2_ref.dtype
    zero = jnp.zeros((), gdt)

    @pl.when(e == 0)
    def _():
        o_ref[...] = jnp.zeros(o_ref.shape, F32)

    def coefs(sub):
        parts = []
        for ii in range(per):
            i1 = (e * nsub + sub) * per + ii
            coef = jnp.zeros((PEER_KEYS, tn), gdt)
            for h in range(PEER_HEADS):
                hs = slice(h * PEER_KEYS, (h + 1) * PEER_KEYS)
                n1 = n1_ref[pl.ds(h * PEER_KEYS + i1, 1), :].astype(gdt)
                cf = cf_ref[pl.ds(h * PEER_KEYS + i1, 1), :].astype(gdt)
                coef = coef + jnp.where(r2_ref[hs, :] < n1, e2_ref[hs, :] * cf, zero)
            parts.append(coef)
        return jnp.concatenate(parts, axis=0).astype(F32)

    def acts(sub):
        return jnp.dot(u_ref[sub * PEER_SUB:(sub + 1) * PEER_SUB, :], xnt_ref[...],
                       preferred_element_type=F32)

    act, coef = acts(0), coefs(0)
    for sub in range(nsub):
        if sub + 1 < nsub:
            act_next, coef_next = acts(sub + 1), coefs(sub + 1)
        pt = (coef * jax.nn.gelu(act)).astype(vt_ref.dtype)
        o_ref[...] += jnp.dot(vt_ref[:, sub * PEER_SUB:(sub + 1) * PEER_SUB], pt,
                              preferred_element_type=F32)
        if sub + 1 < nsub:
            act, coef = act_next, coef_next


def _peerffn(u_tab, v_tab_t, xnt, r2, e2, n1, cf, tn, ec):
    ne, d = u_tab.shape
    n = xnt.shape[1]
    rows = PEER_HEADS * PEER_KEYS
    sel = pl.BlockSpec((rows, tn), lambda i, e: (0, i))
    return pl.pallas_call(
        _peerffn_kernel,
        grid=(n // tn, ne // ec),
        in_specs=[
            pl.BlockSpec((ec, d), lambda i, e: (e, 0)),
            pl.BlockSpec((d, ec), lambda i, e: (0, e)),
            pl.BlockSpec((d, tn), lambda i, e: (0, i)),
            sel, sel, sel, sel,
        ],
        out_specs=pl.BlockSpec((d, tn), lambda i, e: (0, i)),
        out_shape=jax.ShapeDtypeStruct((d, n), F32),
        compiler_params=_params("parallel", "arbitrary"),
        name="peerffn",
    )(u_tab, v_tab_t, xnt, r2, e2, n1, cf)


def _final_kernel(x1_ref, pt_ref, g_ref, o_ref):
    o_ref[...] = _rms(x1_ref[...] + pt_ref[...].T, g_ref[...])


def _final(x1, peer_t, g, tm):
    n, d = x1.shape
    return pl.pallas_call(
        _final_kernel,
        grid=(n // tm,),
        in_specs=[
            pl.BlockSpec((tm, d), lambda i: (i, 0)),
            pl.BlockSpec((d, tm), lambda i: (0, i)),
            pl.BlockSpec((1, d), lambda i: (0, 0)),
        ],
        out_specs=pl.BlockSpec((tm, d), lambda i: (i, 0)),
        out_shape=jax.ShapeDtypeStruct((n, d), F32),
        compiler_params=_params("parallel"),
        name="final",
    )(x1, peer_t, g)


def _rope_tables(seq):
    half = ROPE_DIMS // 2
    pos = jnp.arange(seq, dtype=F32)
    inv_freq = ROPE_THETA ** (-jnp.arange(0, ROPE_DIMS, 2, dtype=F32) / ROPE_DIMS)
    ang = pos[:, None] * inv_freq[None, :]
    cos, sin = jnp.cos(ang), jnp.sin(ang)
    ones = jnp.ones((seq, QK_DIM - ROPE_DIMS), F32)
    zeros = jnp.zeros((seq, half), F32)
    rest = jnp.zeros((seq, QK_DIM - ROPE_DIMS), F32)
    c = jnp.concatenate([cos, cos, ones], axis=1)
    sa = jnp.concatenate([-sin, zeros, rest], axis=1)
    sb = jnp.concatenate([zeros, sin, rest], axis=1)
    return tuple(jnp.concatenate([t, t], axis=1) for t in (c, sa, sb))


def _tile(n, want):
    t = min(n, want)
    assert n % t == 0, (n, t)
    return t


def kernel(x, mix_norm_g, w_in, lambda_q1, lambda_k1, lambda_q2, lambda_k2, subln_g,
           gmlp_z_norm_g, gmlp_w_s, gmlp_b_s, gmlp_out_g, w_out, ffn_norm_g,
           peer_w_q, peer_sub_keys, peer_u, peer_v, final_norm_g):
    b, s, d = x.shape
    n = b * s
    depth = w_in.shape[0]
    cos_t, sa_t, sb_t = _rope_tables(s)
    tm_in = _tile(s, 512)
    tq = _tile(s, 512)
    tm = _tile(n, 512)
    tn = _tile(n, 512)
    ec = 1024

    x2 = x.reshape(n, d)
    for l in range(depth):
        lambda_init = 0.8 - 0.6 * math.exp(-0.3 * l)
        lam = (jnp.exp(jnp.sum(lambda_q1[l] * lambda_k1[l]))
               - jnp.exp(jnp.sum(lambda_q2[l] * lambda_k2[l])) + lambda_init).reshape(1).astype(F32)

        proj = _inproj(x2, mix_norm_g[l][None], w_in[l].astype(MXU_DTYPE), cos_t, sa_t, sb_t,
                       gmlp_z_norm_g[l][None], s, tm_in)
        attn = _attn(lam, proj.reshape(b, s, 5 * WIDTH), subln_g[l][None], 1.0 - lambda_init, tq)
        b_full = jnp.broadcast_to(gmlp_b_s[l][:, :, None], (GROUPS, GMLP_LEN, GROUP_CH))
        gm = _gmlp(proj, gmlp_w_s[l], b_full, gmlp_out_g[l][None], tm)
        x1, xnt = _outproj(attn.reshape(n, WIDTH), gm, x2, w_out[l].astype(MXU_DTYPE),
                           ffn_norm_g[l][None], tm)

        qpt = _peerq(peer_w_q[l].T.astype(MXU_DTYPE), xnt, tn)
        r2, e2, n1, cf = _peersel(qpt, peer_sub_keys[l].astype(MXU_DTYPE), tn)
        peer_t = _peerffn(peer_u[l].astype(MXU_DTYPE), peer_v[l].T.astype(MXU_DTYPE), xnt,
                          r2, e2, n1, cf, tn, ec)
        if l + 1 < depth:
            x2 = x1 + peer_t.T
    return _final(x1, peer_t, final_norm_g[None], tm).reshape(b, s, d)
```

```python
import functools
import math

import jax
import jax.numpy as jnp
from jax import lax
from jax.experimental import pallas as pl
from jax.experimental.pallas import tpu as pltpu

F32 = jnp.float32
MXU_DTYPE = jnp.bfloat16
GATE_DTYPE = jnp.bfloat16

EPS = 1e-6
NEG_INF = -1e30
CHUNK = 64
ROPE_THETA = 500000.0

HEADS = 8
QK_DIM = 64
V_DIM = 128
ROPE_DIMS = QK_DIM // 4
GROUPS = 8
GROUP_CH = 128
GMLP_LEN = 128
WIDTH = 1024

PEER_HEADS = 8
PEER_KEYS = 128
PEER_TOPK = 16

LANES = 128
SUBLANES = 8
MXU_COLS = 256
ATTN_KEY_TILE = 2 * MXU_COLS
Q_SCALE = QK_DIM ** -0.5 * math.log2(math.e)
PEER_SUB = 2 * MXU_COLS
VMEM_LIMIT = 56 * 1024 * 1024


def _params(*sem):
    return pltpu.CompilerParams(dimension_semantics=sem, vmem_limit_bytes=VMEM_LIMIT)


def _rms(x, g):
    return x * lax.rsqrt(jnp.mean(x * x, axis=-1, keepdims=True) + EPS) * g


def _inproj_kernel(x_ref, g_ref, w_ref, cos_ref, sa_ref, sb_ref, zg_ref, o_ref):
    hn = _rms(x_ref[...], g_ref[...]).astype(w_ref.dtype)
    c, sa, sb = cos_ref[...], sa_ref[...], sb_ref[...]
    zg = zg_ref[...]
    for j in range(5):
        acc = jnp.dot(hn, w_ref[:, j * WIDTH:(j + 1) * WIDTH], preferred_element_type=F32)
        for h in range(HEADS):
            t = acc[:, h * LANES:(h + 1) * LANES]
            if j < 2:
                t = t * c + pltpu.roll(t, LANES - 8, 1) * sa + pltpu.roll(t, 8, 1) * sb
                if j == 0:
                    t = t * Q_SCALE
            elif j >= 3:
                t = jax.nn.gelu(t)
                if j == 4:
                    t = _rms(t, zg[:, h * LANES:(h + 1) * LANES])
            o_ref[j * HEADS + h] = t.astype(o_ref.dtype)


def _inproj(x2, g, w, cos_t, sa_t, sb_t, zg, seq, tm):
    n, d = x2.shape
    nseq = seq // tm
    tab = pl.BlockSpec((tm, LANES), lambda i: (i % nseq, 0))
    return pl.pallas_call(
        _inproj_kernel,
        grid=(n // tm,),
        in_specs=[
            pl.BlockSpec((tm, d), lambda i: (i, 0)),
            pl.BlockSpec((1, d), lambda i: (0, 0)),
            pl.BlockSpec((d, 5 * WIDTH), lambda i: (0, 0), pipeline_mode=pl.Buffered(1)),
            tab, tab, tab,
            pl.BlockSpec((1, WIDTH), lambda i: (0, 0)),
        ],
        out_specs=pl.BlockSpec((5 * HEADS, tm, LANES), lambda i: (0, i, 0)),
        out_shape=jax.ShapeDtypeStruct((5 * HEADS, n, LANES), MXU_DTYPE),
        compiler_params=_params("parallel"),
        name="inproj",
    )(x2, g, w, cos_t, sa_t, sb_t, zg)


def _attn_kernel(lam_ref, q_ref, k_ref, v_ref, g_ref, o_ref, vt_ref, qpad_ref, s_ref, m_ref, l_ref,
                 acc_ref, *, tq, tk, post_scale):
    qi = pl.program_id(2)
    nkb = vt_ref.shape[0]
    nsub = tq // tk
    chains = range(2 * nsub)

    @pl.when(qi == 0)
    def _():
        def tr(c, carry):
            off = pl.multiple_of(c * tk, tk)
            vt_ref[c] = v_ref[pl.ds(off, tk), :].astype(F32).T.astype(vt_ref.dtype)
            return carry
        lax.fori_loop(0, nkb, tr, 0)

    qt = q_ref[...].astype(F32).T
    row = lax.broadcasted_iota(jnp.int32, qt.shape, 0)
    qpad_ref[...] = jnp.concatenate([jnp.where(row < QK_DIM, qt, 0.0),
                                     jnp.where(row >= QK_DIM, qt, 0.0)], axis=1).astype(qpad_ref.dtype)

    m_ref[...] = jnp.full(m_ref.shape, NEG_INF, F32)
    l_ref[...] = jnp.zeros(l_ref.shape, F32)
    acc_ref[...] = jnp.zeros(acc_ref.shape, F32)

    def cols(c):
        return slice(c * tk, (c + 1) * tk)

    def scores(kb, c):
        return jnp.dot(kb, qpad_ref[:, cols(c)], preferred_element_type=F32)

    def consume(s, vb, c, masked):
        cs = cols(c)
        if masked:
            kc = lax.broadcasted_iota(jnp.int32, s.shape, 0) // CHUNK
            qc = lax.broadcasted_iota(jnp.int32, s.shape, 1) // CHUNK
            s = jnp.where(kc <= qc, s, NEG_INF)
        m_old = m_ref[:, cs]
        m_new = jnp.maximum(m_old, jnp.max(s, axis=0, keepdims=True))
        alpha = jnp.exp2(m_old - m_new)
        p = jnp.exp2(s - m_new)
        l_ref[:, cs] = alpha * l_ref[:, cs] + jnp.sum(p, axis=0, keepdims=True)
        acc_ref[:, cs] = alpha * acc_ref[:, cs] + jnp.dot(
            vb, p.astype(vb.dtype), preferred_element_type=F32)
        m_ref[:, cs] = m_new

    kb0 = k_ref[pl.ds(0, tk), :]
    for c in chains:
        s_ref[:, cols(c)] = scores(kb0, c)

    def body(j, carry):
        off = pl.multiple_of((j + 1) * tk, tk)
        kb = k_ref[pl.ds(off, tk), :]
        vb = vt_ref[j]
        for c in chains:
            s_cur = s_ref[:, cols(c)]
            s_next = scores(kb, c)
            consume(s_cur, vb, c, False)
            s_ref[:, cols(c)] = s_next
        return carry
    nfull = qi * nsub
    lax.fori_loop(0, nfull, body, 0)

    for d in range(nsub):
        vb = vt_ref[nfull + d]
        if d + 1 < nsub:
            off = pl.multiple_of((nfull + d + 1) * tk, tk)
            kb = k_ref[pl.ds(off, tk), :]
        for c in chains:
            u = c % nsub
            if u < d:
                continue
            s_cur = s_ref[:, cols(c)]
            if u > d:
                s_next = scores(kb, c)
            consume(s_cur, vb, c, u == d)
            if u > d:
                s_ref[:, cols(c)] = s_next

    inv = 1.0 / l_ref[...]
    o = acc_ref[...] * inv
    ot = o[:, :tq] - lam_ref[0] * o[:, tq:]
    o_ref[...] = (_rms(ot.T, g_ref[...]) * post_scale).astype(o_ref.dtype)


def _attn(lam, proj, subln_g, post_scale, batch, tq, tk):
    n = proj.shape[1]
    s = n // batch
    nq = s // tq
    kern = functools.partial(_attn_kernel, tq=tq, tk=tk, post_scale=post_scale)
    return pl.pallas_call(
        kern,
        grid=(batch, HEADS, nq),
        in_specs=[
            pl.BlockSpec(memory_space=pltpu.SMEM),
            pl.BlockSpec((None, tq, LANES), lambda bi, h, qi: (h, bi * nq + qi, 0)),
            pl.BlockSpec((None, s, LANES), lambda bi, h, qi: (HEADS + h, bi, 0)),
            pl.BlockSpec((None, s, LANES), lambda bi, h, qi: (2 * HEADS + h, bi, 0)),
            pl.BlockSpec((1, V_DIM), lambda bi, h, qi: (0, 0)),
        ],
        out_specs=pl.BlockSpec((tq, LANES), lambda bi, h, qi: (bi * nq + qi, h)),
        out_shape=jax.ShapeDtypeStruct((n, WIDTH), MXU_DTYPE),
        scratch_shapes=[
            pltpu.VMEM((s // tk, V_DIM, tk), MXU_DTYPE),
            pltpu.VMEM((2 * QK_DIM, 2 * tq), MXU_DTYPE),
            pltpu.VMEM((tk, 2 * tq), F32),
            pltpu.VMEM((1, 2 * tq), F32),
            pltpu.VMEM((1, 2 * tq), F32),
            pltpu.VMEM((V_DIM, 2 * tq), F32),
        ],
        compiler_params=_params("parallel", "parallel", "arbitrary"),
        name="attn",
    )(lam, proj, proj, proj, subln_g)


def _gmlp_kernel(u_ref, z_ref, w_ref, b_ref, g_ref, o_ref, gm_ref):
    tm = u_ref.shape[1]
    pc = lax.broadcasted_iota(jnp.int32, (GMLP_LEN, GMLP_LEN), 0) // CHUNK
    qc = lax.broadcasted_iota(jnp.int32, (GMLP_LEN, GMLP_LEN), 1) // CHUNK
    for g in range(GROUPS):
        w = jnp.where(pc >= qc, w_ref[g], 0.0).astype(z_ref.dtype)
        bias = b_ref[g]
        cs = slice(g * GROUP_CH, (g + 1) * GROUP_CH)
        for nb in range(tm // GMLP_LEN):
            rs = slice(nb * GMLP_LEN, (nb + 1) * GMLP_LEN)
            sz = jnp.dot(w, z_ref[g, rs, :], preferred_element_type=F32) + bias
            gm_ref[rs, cs] = u_ref[g, rs, :].astype(F32) * sz
    o_ref[...] = _rms(gm_ref[...], g_ref[...]).astype(o_ref.dtype)


def _gmlp(proj, w_s, b_full, out_g, tm):
    n = proj.shape[1]
    return pl.pallas_call(
        _gmlp_kernel,
        grid=(n // tm,),
        in_specs=[
            pl.BlockSpec((GROUPS, tm, GROUP_CH), lambda i: (3, i, 0)),
            pl.BlockSpec((GROUPS, tm, GROUP_CH), lambda i: (4, i, 0)),
            pl.BlockSpec((GROUPS, GMLP_LEN, GMLP_LEN), lambda i: (0, 0, 0)),
            pl.BlockSpec((GROUPS, GMLP_LEN, GROUP_CH), lambda i: (0, 0, 0)),
            pl.BlockSpec((1, WIDTH), lambda i: (0, 0)),
        ],
        out_specs=pl.BlockSpec((tm, WIDTH), lambda i: (i, 0)),
        out_shape=jax.ShapeDtypeStruct((n, WIDTH), MXU_DTYPE),
        scratch_shapes=[pltpu.VMEM((tm, WIDTH), F32)],
        compiler_params=_params("parallel"),
        name="gmlp",
    )(proj, proj, w_s, b_full, out_g)


def _outproj_kernel(a_ref, gm_ref, x_ref, w_ref, g_ref, x1_ref, xnt_ref):
    mixed = (jnp.dot(a_ref[...], w_ref[:WIDTH, :], preferred_element_type=F32)
             + jnp.dot(gm_ref[...], w_ref[WIDTH:, :], preferred_element_type=F32))
    x1 = x_ref[...] + mixed
    x1_ref[...] = x1
    xnt_ref[...] = _rms(x1, g_ref[...]).T.astype(xnt_ref.dtype)


def _outproj(attn2, gm, x2, w_out, ffn_g, tm):
    n, d = x2.shape
    return pl.pallas_call(
        _outproj_kernel,
        grid=(n // tm,),
        in_specs=[
            pl.BlockSpec((tm, WIDTH), lambda i: (i, 0)),
            pl.BlockSpec((tm, WIDTH), lambda i: (i, 0)),
            pl.BlockSpec((tm, d), lambda i: (i, 0)),
            pl.BlockSpec((2 * WIDTH, d), lambda i: (0, 0)),
            pl.BlockSpec((1, d), lambda i: (0, 0)),
        ],
        out_specs=[
            pl.BlockSpec((tm, d), lambda i: (i, 0)),
            pl.BlockSpec((d, tm), lambda i: (0, i)),
        ],
        out_shape=[
            jax.ShapeDtypeStruct((n, d), F32),
            jax.ShapeDtypeStruct((d, n), MXU_DTYPE),
        ],
        compiler_params=_params("parallel"),
        name="outproj",
    )(attn2, gm, x2, w_out, ffn_g)


def _peerq_kernel(w_ref, xnt_ref, o_ref):
    o_ref[...] = jnp.dot(w_ref[...], xnt_ref[...], preferred_element_type=F32).astype(o_ref.dtype)


def _peerq(w_qt, xnt, tn):
    dq, d = w_qt.shape
    n = xnt.shape[1]
    return pl.pallas_call(
        _peerq_kernel,
        grid=(n // tn,),
        in_specs=[
            pl.BlockSpec((dq, d), lambda i: (0, 0)),
            pl.BlockSpec((d, tn), lambda i: (0, i)),
        ],
        out_specs=pl.BlockSpec((dq, tn), lambda i: (0, i)),
        out_shape=jax.ShapeDtypeStruct((dq, n), MXU_DTYPE),
        compiler_params=_params("parallel"),
        name="peerq",
    )(w_qt, xnt)


def _top_desc(x, count):
    rows = []
    rank = jnp.full(x.shape, PEER_KEYS - 1, F32)
    for r in range(count):
        m = jnp.max(x, axis=0, keepdims=True)
        rows.append(m)
        hit = x == m
        rank = jnp.where(hit, float(r), rank)
        x = jnp.where(hit, -jnp.inf, x)
    return rows, rank


def _peersel_kernel(q_ref, k_ref, r2_ref, e2_ref, n1_ref, cf_ref):
    nk = PEER_KEYS
    s1 = jnp.dot(k_ref[0], q_ref[:nk, :], preferred_element_type=F32)
    s2 = jnp.dot(k_ref[1], q_ref[nk:, :], preferred_element_type=F32)
    for c in range(s1.shape[1] // LANES):
        cs = slice(c * LANES, (c + 1) * LANES)
        r2, e2, n1, cf = _peersel_column(s1[:, cs], s2[:, cs])
        r2_ref[:, cs] = r2.astype(r2_ref.dtype)
        e2_ref[:, cs] = e2.astype(e2_ref.dtype)
        n1_ref[:, cs] = n1
        cf_ref[:, cs] = cf


def _peersel_column(s1, s2):
    a1, _ = _top_desc(s1, PEER_TOPK + 1)
    a2, rank2 = _top_desc(s2, PEER_TOPK + 1)

    a2lo = jnp.concatenate(a2[:8], axis=0)
    sub = lax.broadcasted_iota(jnp.int32, a2lo.shape, 0)
    blocks = [a1[0] + a2lo, a1[0] + jnp.concatenate(a2[8:16], axis=0)]
    for r in range(1, 8):
        blocks.append(a1[r] + jnp.where(sub < (PEER_TOPK + 1) // (r + 1), a2lo, -jnp.inf))
    blocks.append(jnp.concatenate(a1[8:16], axis=0) + a2[0])
    tail = jnp.where(sub == 0, a1[0] + a2[PEER_TOPK],
                     jnp.where(sub == 1, a1[PEER_TOPK] + a2[0], -jnp.inf))
    blocks.append(tail)
    best, _ = _top_desc(jnp.concatenate(blocks, axis=0), PEER_TOPK + 1)

    zsum = jnp.zeros_like(best[0])
    for kk in range(PEER_TOPK):
        zsum = zsum + jnp.exp(best[kk] - best[0])
    thr = 0.5 * (best[PEER_TOPK - 1] + best[PEER_TOPK]) - s1
    n1 = jnp.zeros_like(s1)
    for c in range(PEER_TOPK + 1):
        n1 = n1 + jnp.where(a2[c] >= thr, 1.0, 0.0)
    return rank2, jnp.exp(s2 - a2[0]), n1, jnp.exp(s1 - a1[0]) / zsum


def _peersel(qpt, keys, tn):
    n = qpt.shape[1]
    rows = PEER_HEADS * PEER_KEYS
    out = pl.BlockSpec((PEER_KEYS, tn), lambda i, h: (h, i))
    shape = jax.ShapeDtypeStruct((rows, n), F32)
    gshape = jax.ShapeDtypeStruct((rows, n), GATE_DTYPE)
    return pl.pallas_call(
        _peersel_kernel,
        grid=(n // tn, PEER_HEADS),
        in_specs=[
            pl.BlockSpec((2 * PEER_KEYS, tn), lambda i, h: (h, i)),
            pl.BlockSpec((None, 2, PEER_KEYS, PEER_KEYS), lambda i, h: (h, 0, 0, 0)),
        ],
        out_specs=[out, out, out, out],
        out_shape=[gshape, gshape, shape, shape],
        compiler_params=_params("parallel", "parallel"),
        name="peersel",
    )(qpt, keys)


def _peerffn_kernel(u_ref, vt_ref, xnt_ref, r2_ref, e2_ref, n1_ref, cf_ref, o_ref, coef0_ref):
    e = pl.program_id(1)
    ne = pl.num_programs(1)
    ec = u_ref.shape[0]
    nsub = ec // PEER_SUB
    per = PEER_SUB // PEER_KEYS
    tn = xnt_ref.shape[1]
    gdt = r2_ref.dtype
    zero = jnp.zeros((), gdt)
    rows = SUBLANES * (4 // jnp.dtype(gdt).itemsize)
    tile = (rows, tn)
    tiled = (PEER_KEYS // rows, rows, tn)

    def coefs(step, sub):
        parts = []
        for ii in range(per):
            k = sub * per + ii
            coef = jnp.zeros(tiled, gdt)
            for h in range(PEER_HEADS):
                hs = slice(h * PEER_KEYS, (h + 1) * PEER_KEYS)
                base = pl.multiple_of(h * PEER_KEYS + step * (nsub * per) + (k // SUBLANES) * SUBLANES,
                                      SUBLANES)
                ks = slice(k % SUBLANES, k % SUBLANES + 1)
                n1 = jnp.broadcast_to(n1_ref[pl.ds(base, SUBLANES), :][ks], tile).astype(gdt)
                cf = jnp.broadcast_to(cf_ref[pl.ds(base, SUBLANES), :][ks], tile).astype(gdt)
                r2 = r2_ref[hs, :].reshape(tiled)
                e2 = e2_ref[hs, :].reshape(tiled)
                coef = coef + jnp.where(r2 < n1[None], e2 * cf[None], zero)
            parts.append(coef.reshape(PEER_KEYS, tn))
        return jnp.concatenate(parts, axis=0)

    @pl.when(e == 0)
    def _():
        o_ref[...] = jnp.zeros(o_ref.shape, F32)
        coef0_ref[...] = coefs(e, 0)

    def acts(sub):
        return jnp.dot(u_ref[sub * PEER_SUB:(sub + 1) * PEER_SUB, :], xnt_ref[...],
                       preferred_element_type=F32)

    coef, act = coef0_ref[...], acts(0)
    for sub in range(nsub):
        if sub + 1 < nsub:
            coef_next, act_next = coefs(e, sub + 1), acts(sub + 1)
        pt = (coef.astype(F32) * jax.nn.gelu(act)).astype(vt_ref.dtype)
        o_ref[...] += jnp.dot(vt_ref[:, sub * PEER_SUB:(sub + 1) * PEER_SUB], pt,
                              preferred_element_type=F32)
        if sub + 1 < nsub:
            act, coef = act_next, coef_next
    coef0_ref[...] = coefs(jnp.minimum(e + 1, ne - 1), 0)


def _peerffn(u_tab, v_tab_t, xnt, r2, e2, n1, cf, tn, ec):
    ne, d = u_tab.shape
    n = xnt.shape[1]
    rows = PEER_HEADS * PEER_KEYS
    assert ec % (SUBLANES * PEER_KEYS) == 0 and ec % PEER_SUB == 0, ec
    sel = pl.BlockSpec((rows, tn), lambda i, e: (0, i))
    return pl.pallas_call(
        _peerffn_kernel,
        grid=(n // tn, ne // ec),
        in_specs=[
            pl.BlockSpec((ec, d), lambda i, e: (e, 0)),
            pl.BlockSpec((d, ec), lambda i, e: (0, e)),
            pl.BlockSpec((d, tn), lambda i, e: (0, i)),
            sel, sel, sel, sel,
        ],
        out_specs=pl.BlockSpec((d, tn), lambda i, e: (0, i)),
        out_shape=jax.ShapeDtypeStruct((d, n), F32),
        scratch_shapes=[pltpu.VMEM((PEER_SUB, tn), r2.dtype)],
        compiler_params=_params("parallel", "arbitrary"),
        name="peerffn",
    )(u_tab, v_tab_t, xnt, r2, e2, n1, cf)


def _final_kernel(x1_ref, pt_ref, g_ref, o_ref):
    o_ref[...] = _rms(x1_ref[...] + pt_ref[...].T, g_ref[...])


def _final(x1, peer_t, g, tm):
    n, d = x1.shape
    return pl.pallas_call(
        _final_kernel,
        grid=(n // tm,),
        in_specs=[
            pl.BlockSpec((tm, d), lambda i: (i, 0)),
            pl.BlockSpec((d, tm), lambda i: (0, i)),
            pl.BlockSpec((1, d), lambda i: (0, 0)),
        ],
        out_specs=pl.BlockSpec((tm, d), lambda i: (i, 0)),
        out_shape=jax.ShapeDtypeStruct((n, d), F32),
        compiler_params=_params("parallel"),
        name="final",
    )(x1, peer_t, g)


def _rope_tables(seq):
    half = ROPE_DIMS // 2
    pos = jnp.arange(seq, dtype=F32)
    inv_freq = ROPE_THETA ** (-jnp.arange(0, ROPE_DIMS, 2, dtype=F32) / ROPE_DIMS)
    ang = pos[:, None] * inv_freq[None, :]
    cos, sin = jnp.cos(ang), jnp.sin(ang)
    ones = jnp.ones((seq, QK_DIM - ROPE_DIMS), F32)
    zeros = jnp.zeros((seq, half), F32)
    rest = jnp.zeros((seq, QK_DIM - ROPE_DIMS), F32)
    c = jnp.concatenate([cos, cos, ones], axis=1)
    sa = jnp.concatenate([-sin, zeros, rest], axis=1)
    sb = jnp.concatenate([zeros, sin, rest], axis=1)
    return tuple(jnp.concatenate([t, t], axis=1) for t in (c, sa, sb))


def _tile(n, want):
    t = min(n, want)
    assert n % t == 0, (n, t)
    return t


def kernel(x, mix_norm_g, w_in, lambda_q1, lambda_k1, lambda_q2, lambda_k2, subln_g,
           gmlp_z_norm_g, gmlp_w_s, gmlp_b_s, gmlp_out_g, w_out, ffn_norm_g,
           peer_w_q, peer_sub_keys, peer_u, peer_v, final_norm_g):
    b, s, d = x.shape
    n = b * s
    depth = w_in.shape[0]
    cos_t, sa_t, sb_t = _rope_tables(s)
    tm_in = _tile(s, 512)
    tk = _tile(s, ATTN_KEY_TILE)
    tq = _tile(s, 2 * tk)
    tm = _tile(n, 512)
    tn = _tile(n, 512)
    ec = 1024

    x2 = x.reshape(n, d)
    for l in range(depth):
        lambda_init = 0.8 - 0.6 * math.exp(-0.3 * l)
        lam = (jnp.exp(jnp.sum(lambda_q1[l] * lambda_k1[l]))
               - jnp.exp(jnp.sum(lambda_q2[l] * lambda_k2[l])) + lambda_init).reshape(1).astype(F32)

        proj = _inproj(x2, mix_norm_g[l][None], w_in[l].astype(MXU_DTYPE), cos_t, sa_t, sb_t,
                       gmlp_z_norm_g[l][None], s, tm_in)
        attn = _attn(lam, proj, subln_g[l][None], 1.0 - lambda_init, b, tq, tk)
        b_full = jnp.broadcast_to(gmlp_b_s[l][:, :, None], (GROUPS, GMLP_LEN, GROUP_CH))
        gm = _gmlp(proj, gmlp_w_s[l], b_full, gmlp_out_g[l][None], tm)
        x1, xnt = _outproj(attn, gm, x2, w_out[l].astype(MXU_DTYPE), ffn_norm_g[l][None], tm)

        qpt = _peerq(peer_w_q[l].T.astype(MXU_DTYPE), xnt, tn)
        r2, e2, n1, cf = _peersel(qpt, peer_sub_keys[l].astype(MXU_DTYPE), tn)
        peer_t = _peerffn(peer_u[l].astype(MXU_DTYPE), peer_v[l].T.astype(MXU_DTYPE), xnt,
                          r2, e2, n1, cf, tn, ec)
        if l + 1 < depth:
            x2 = x1 + peer_t.T
    return _final(x1, peer_t, final_norm_g[None], tm).reshape(b, s, d)
```

```python
import functools
import math

import jax
import jax.numpy as jnp
from jax import lax
from jax.experimental import pallas as pl
from jax.experimental.pallas import tpu as pltpu

F32 = jnp.float32
MXU_DTYPE = jnp.bfloat16
GATE_DTYPE = jnp.bfloat16

EPS = 1e-6
NEG_INF = -1e30
CHUNK = 64
ROPE_THETA = 500000.0

HEADS = 8
QK_DIM = 64
V_DIM = 128
ROPE_DIMS = QK_DIM // 4
GROUPS = 8
GROUP_CH = 128
GMLP_LEN = 128
WIDTH = 1024

PEER_HEADS = 8
PEER_KEYS = 128
PEER_TOPK = 16

LANES = 128
SUBLANES = 8
MXU_COLS = 256
ATTN_KEY_TILE = 2 * MXU_COLS
Q_SCALE = QK_DIM ** -0.5 * math.log2(math.e)
PEER_SUB = 2 * MXU_COLS
VMEM_LIMIT = 56 * 1024 * 1024


def _params(*sem):
    return pltpu.CompilerParams(dimension_semantics=sem, vmem_limit_bytes=VMEM_LIMIT)


def _rms(x, g):
    return x * lax.rsqrt(jnp.mean(x * x, axis=-1, keepdims=True) + EPS) * g


def _inproj_kernel(x_ref, g_ref, w_ref, cos_ref, sa_ref, sb_ref, zg_ref, o_ref):
    hn = _rms(x_ref[...], g_ref[...]).astype(w_ref.dtype)
    c, sa, sb = cos_ref[...], sa_ref[...], sb_ref[...]
    zg = zg_ref[...]
    for j in range(5):
        acc = jnp.dot(hn, w_ref[:, j * WIDTH:(j + 1) * WIDTH], preferred_element_type=F32)
        for h in range(HEADS):
            t = acc[:, h * LANES:(h + 1) * LANES]
            if j < 2:
                t = t * c + pltpu.roll(t, LANES - 8, 1) * sa + pltpu.roll(t, 8, 1) * sb
                if j == 0:
                    t = t * Q_SCALE
            elif j >= 3:
                t = jax.nn.gelu(t)
                if j == 4:
                    t = _rms(t, zg[:, h * LANES:(h + 1) * LANES])
            o_ref[j * HEADS + h] = t.astype(o_ref.dtype)


def _inproj(x2, g, w, cos_t, sa_t, sb_t, zg, seq, tm):
    n, d = x2.shape
    nseq = seq // tm
    tab = pl.BlockSpec((tm, LANES), lambda i: (i % nseq, 0))
    return pl.pallas_call(
        _inproj_kernel,
        grid=(n // tm,),
        in_specs=[
            pl.BlockSpec((tm, d), lambda i: (i, 0)),
            pl.BlockSpec((1, d), lambda i: (0, 0)),
            pl.BlockSpec((d, 5 * WIDTH), lambda i: (0, 0), pipeline_mode=pl.Buffered(1)),
            tab, tab, tab,
            pl.BlockSpec((1, WIDTH), lambda i: (0, 0)),
        ],
        out_specs=pl.BlockSpec((5 * HEADS, tm, LANES), lambda i: (0, i, 0)),
        out_shape=jax.ShapeDtypeStruct((5 * HEADS, n, LANES), MXU_DTYPE),
        compiler_params=_params("parallel"),
        name="inproj",
    )(x2, g, w, cos_t, sa_t, sb_t, zg)


def _attn_kernel(lam_ref, q_ref, k_ref, v_ref, g_ref, o_ref, vt_ref, qpad_ref, s_ref, m_ref, l_ref,
                 acc_ref, *, tq, tk, post_scale):
    qi = pl.program_id(2)
    nkb = vt_ref.shape[0]
    nsub = tq // tk
    chains = range(2 * nsub)

    @pl.when(qi == 0)
    def _():
        def tr(c, carry):
            off = pl.multiple_of(c * tk, tk)
            vt_ref[c] = v_ref[pl.ds(off, tk), :].astype(F32).T.astype(vt_ref.dtype)
            return carry
        lax.fori_loop(0, nkb, tr, 0)

    qt = q_ref[...].astype(F32).T
    row = lax.broadcasted_iota(jnp.int32, qt.shape, 0)
    qpad_ref[...] = jnp.concatenate([jnp.where(row < QK_DIM, qt, 0.0),
                                     jnp.where(row >= QK_DIM, qt, 0.0)], axis=1).astype(qpad_ref.dtype)

    m_ref[...] = jnp.full(m_ref.shape, NEG_INF, F32)
    l_ref[...] = jnp.zeros(l_ref.shape, F32)
    acc_ref[...] = jnp.zeros(acc_ref.shape, F32)

    def cols(c):
        return slice(c * tk, (c + 1) * tk)

    def scores(kb, c):
        return jnp.dot(kb, qpad_ref[:, cols(c)], preferred_element_type=F32)

    def consume(s, vb, c, masked):
        cs = cols(c)
        if masked:
            kc = lax.broadcasted_iota(jnp.int32, s.shape, 0) // CHUNK
            qc = lax.broadcasted_iota(jnp.int32, s.shape, 1) // CHUNK
            s = jnp.where(kc <= qc, s, NEG_INF)
        m_old = m_ref[:, cs]
        m_new = jnp.maximum(m_old, jnp.max(s, axis=0, keepdims=True))
        alpha = jnp.exp2(m_old - m_new)
        p = jnp.exp2(s - m_new)
        l_ref[:, cs] = alpha * l_ref[:, cs] + jnp.sum(p, axis=0, keepdims=True)
        acc_ref[:, cs] = alpha * acc_ref[:, cs] + jnp.dot(
            vb, p.astype(vb.dtype), preferred_element_type=F32)
        m_ref[:, cs] = m_new

    kb0 = k_ref[pl.ds(0, tk), :]
    for c in chains:
        s_ref[:, cols(c)] = scores(kb0, c)

    def body(j, carry):
        off = pl.multiple_of((j + 1) * tk, tk)
        kb = k_ref[pl.ds(off, tk), :]
        vb = vt_ref[j]
        for c in chains:
            s_cur = s_ref[:, cols(c)]
            s_next = scores(kb, c)
            consume(s_cur, vb, c, False)
            s_ref[:, cols(c)] = s_next
        return carry
    nfull = qi * nsub
    lax.fori_loop(0, nfull, body, 0)

    for d in range(nsub):
        vb = vt_ref[nfull + d]
        if d + 1 < nsub:
            off = pl.multiple_of((nfull + d + 1) * tk, tk)
            kb = k_ref[pl.ds(off, tk), :]
        for c in chains:
            u = c % nsub
            if u < d:
                continue
            s_cur = s_ref[:, cols(c)]
            if u > d:
                s_next = scores(kb, c)
            consume(s_cur, vb, c, u == d)
            if u > d:
                s_ref[:, cols(c)] = s_next

    inv = 1.0 / l_ref[...]
    o = acc_ref[...] * inv
    ot = o[:, :tq] - lam_ref[0] * o[:, tq:]
    o_ref[...] = (_rms(ot.T, g_ref[...]) * post_scale).astype(o_ref.dtype)


def _attn(lam, proj, subln_g, post_scale, batch, tq, tk):
    n = proj.shape[1]
    s = n // batch
    nq = s // tq
    kern = functools.partial(_attn_kernel, tq=tq, tk=tk, post_scale=post_scale)
    return pl.pallas_call(
        kern,
        grid=(batch, HEADS, nq),
        in_specs=[
            pl.BlockSpec(memory_space=pltpu.SMEM),
            pl.BlockSpec((None, tq, LANES), lambda bi, h, qi: (h, bi * nq + qi, 0)),
            pl.BlockSpec((None, s, LANES), lambda bi, h, qi: (HEADS + h, bi, 0)),
            pl.BlockSpec((None, s, LANES), lambda bi, h, qi: (2 * HEADS + h, bi, 0)),
            pl.BlockSpec((1, V_DIM), lambda bi, h, qi: (0, 0)),
        ],
        out_specs=pl.BlockSpec((tq, LANES), lambda bi, h, qi: (bi * nq + qi, h)),
        out_shape=jax.ShapeDtypeStruct((n, WIDTH), MXU_DTYPE),
        scratch_shapes=[
            pltpu.VMEM((s // tk, V_DIM, tk), MXU_DTYPE),
            pltpu.VMEM((2 * QK_DIM, 2 * tq), MXU_DTYPE),
            pltpu.VMEM((tk, 2 * tq), F32),
            pltpu.VMEM((1, 2 * tq), F32),
            pltpu.VMEM((1, 2 * tq), F32),
            pltpu.VMEM((V_DIM, 2 * tq), F32),
        ],
        compiler_params=_params("parallel", "parallel", "arbitrary"),
        name="attn",
    )(lam, proj, proj, proj, subln_g)


def _gmlp_kernel(u_ref, z_ref, w_ref, b_ref, g_ref, o_ref, gm_ref):
    tm = u_ref.shape[1]
    pc = lax.broadcasted_iota(jnp.int32, (GMLP_LEN, GMLP_LEN), 0) // CHUNK
    qc = lax.broadcasted_iota(jnp.int32, (GMLP_LEN, GMLP_LEN), 1) // CHUNK
    for g in range(GROUPS):
        w = jnp.where(pc >= qc, w_ref[g], 0.0).astype(z_ref.dtype)
        bias = b_ref[g]
        cs = slice(g * GROUP_CH, (g + 1) * GROUP_CH)
        for nb in range(tm // GMLP_LEN):
            rs = slice(nb * GMLP_LEN, (nb + 1) * GMLP_LEN)
            sz = jnp.dot(w, z_ref[g, rs, :], preferred_element_type=F32) + bias
            gm_ref[rs, cs] = u_ref[g, rs, :].astype(F32) * sz
    o_ref[...] = _rms(gm_ref[...], g_ref[...]).astype(o_ref.dtype)


def _gmlp(proj, w_s, b_full, out_g, tm):
    n = proj.shape[1]
    return pl.pallas_call(
        _gmlp_kernel,
        grid=(n // tm,),
        in_specs=[
            pl.BlockSpec((GROUPS, tm, GROUP_CH), lambda i: (3, i, 0)),
            pl.BlockSpec((GROUPS, tm, GROUP_CH), lambda i: (4, i, 0)),
            pl.BlockSpec((GROUPS, GMLP_LEN, GMLP_LEN), lambda i: (0, 0, 0)),
            pl.BlockSpec((GROUPS, GMLP_LEN, GROUP_CH), lambda i: (0, 0, 0)),
            pl.BlockSpec((1, WIDTH), lambda i: (0, 0)),
        ],
        out_specs=pl.BlockSpec((tm, WIDTH), lambda i: (i, 0)),
        out_shape=jax.ShapeDtypeStruct((n, WIDTH), MXU_DTYPE),
        scratch_shapes=[pltpu.VMEM((tm, WIDTH), F32)],
        compiler_params=_params("parallel"),
        name="gmlp",
    )(proj, proj, w_s, b_full, out_g)


def _outproj_kernel(a_ref, gm_ref, x_ref, w_ref, g_ref, x1_ref, xnt_ref):
    mixed = (jnp.dot(a_ref[...], w_ref[:WIDTH, :], preferred_element_type=F32)
             + jnp.dot(gm_ref[...], w_ref[WIDTH:, :], preferred_element_type=F32))
    x1 = x_ref[...] + mixed
    x1_ref[...] = x1
    xnt_ref[...] = _rms(x1, g_ref[...]).T.astype(xnt_ref.dtype)


def _outproj(attn2, gm, x2, w_out, ffn_g, tm):
    n, d = x2.shape
    return pl.pallas_call(
        _outproj_kernel,
        grid=(n // tm,),
        in_specs=[
            pl.BlockSpec((tm, WIDTH), lambda i: (i, 0)),
            pl.BlockSpec((tm, WIDTH), lambda i: (i, 0)),
            pl.BlockSpec((tm, d), lambda i: (i, 0)),
            pl.BlockSpec((2 * WIDTH, d), lambda i: (0, 0)),
            pl.BlockSpec((1, d), lambda i: (0, 0)),
        ],
        out_specs=[
            pl.BlockSpec((tm, d), lambda i: (i, 0)),
            pl.BlockSpec((d, tm), lambda i: (0, i)),
        ],
        out_shape=[
            jax.ShapeDtypeStruct((n, d), F32),
            jax.ShapeDtypeStruct((d, n), MXU_DTYPE),
        ],
        compiler_params=_params("parallel"),
        name="outproj",
    )(attn2, gm, x2, w_out, ffn_g)


def _sort_desc(v):
    v, n, k = list(v), len(v), 2
    while k <= n:
        j = k // 2
        while j >= 1:
            for i in range(n):
                p = i ^ j
                if p > i:
                    hi, lo = jnp.maximum(v[i], v[p]), jnp.minimum(v[i], v[p])
                    v[i], v[p] = (hi, lo) if (i & k) == 0 else (lo, hi)
            j //= 2
        k *= 2
    return v


def _merge_top(lists, singles, count):
    lists, singles, rows = list(lists), list(singles), []
    for r in range(count):
        head = lists[0]
        for s in singles:
            head = jnp.maximum(head, s) if s.shape == head.shape else head
        m = jnp.max(head, axis=0, keepdims=True)
        for s in singles:
            if s.shape != head.shape:
                m = jnp.maximum(m, s)
        rows.append(m)
        if r + 1 == count:
            break
        hit = lists[0] == m
        for k in range(min(len(lists), count - r - 1)):
            nxt = lists[k + 1] if k + 1 < len(lists) else -jnp.inf
            lists[k] = jnp.where(hit, nxt, lists[k])
        singles = [jnp.where(s == m, -jnp.inf, s) for s in singles]
    return rows


def _vregs(x):
    return [x[k * SUBLANES:(k + 1) * SUBLANES, :] for k in range(x.shape[0] // SUBLANES)]


def _peersel_kernel(w_ref, k_ref, xnt_ref, r2_ref, e2_ref, n1_ref, cf_ref):
    nk = PEER_KEYS

    def scores(h):
        q = jnp.dot(w_ref[h * 2 * nk:(h + 1) * 2 * nk, :], xnt_ref[...],
                    preferred_element_type=F32).astype(xnt_ref.dtype)
        return (jnp.dot(k_ref[h, 0], q[:nk], preferred_element_type=F32),
                jnp.dot(k_ref[h, 1], q[nk:], preferred_element_type=F32))

    s = scores(0)
    for h in range(PEER_HEADS):
        if h + 1 < PEER_HEADS:
            s_next = scores(h + 1)
        rows = slice(h * nk, (h + 1) * nk)
        for c in range(xnt_ref.shape[1] // LANES):
            cs = slice(c * LANES, (c + 1) * LANES)
            r2, e2, n1, cf = _peersel_column(s[0][:, cs], s[1][:, cs])
            r2_ref[rows, cs] = r2.astype(r2_ref.dtype)
            e2_ref[rows, cs] = e2.astype(e2_ref.dtype)
            n1_ref[rows, cs] = n1
            cf_ref[rows, cs] = cf
        if h + 1 < PEER_HEADS:
            s = s_next


def _peersel_column(s1, s2):
    top = PEER_TOPK + 1
    a1 = _merge_top(_sort_desc(_vregs(s1)), [], top)
    a2 = _merge_top(_sort_desc(_vregs(s2)), [], top)

    a1lo = jnp.concatenate(a1[:SUBLANES], axis=0)
    sub = lax.broadcasted_iota(jnp.int32, a1lo.shape, 0)
    lists = [jnp.where(sub < min(SUBLANES, top // (c + 1)), a1lo + a2[c], -jnp.inf)
             for c in range(top)]
    singles = [jnp.concatenate(a1[SUBLANES:2 * SUBLANES], axis=0) + a2[0], a1[PEER_TOPK] + a2[0]]
    best = _merge_top(lists, singles, top)

    zsum = jnp.zeros_like(best[0])
    for kk in range(PEER_TOPK):
        zsum = zsum + jnp.exp(best[kk] - best[0])
    tau = 0.5 * (best[PEER_TOPK - 1] + best[PEER_TOPK])
    rank2 = jnp.full(s2.shape, PEER_KEYS - 1, F32)
    for c in reversed(range(top)):
        rank2 = jnp.where(s2 >= a2[c], float(c), rank2)
    thr = tau - s1
    n1 = jnp.zeros_like(s1)
    for c in range(SUBLANES):
        n1 = jnp.where(a2[c] >= thr, float(c + 1), n1)
    thr_top = tau - a1[0]
    n_top = jnp.zeros_like(thr_top)
    for c in range(SUBLANES, top):
        n_top = jnp.where(a2[c] >= thr_top, float(c + 1), n_top)
    n1 = jnp.where(s1 == a1[0], jnp.maximum(n1, n_top), n1)
    return rank2, jnp.exp(s2 - a2[0]), n1, jnp.exp(s1 - a1[0]) * (0.5 / zsum)


def _peersel(w_qt, keys, xnt, tn):
    dq, d = w_qt.shape
    n = xnt.shape[1]
    rows = PEER_HEADS * PEER_KEYS
    out = pl.BlockSpec((rows, tn), lambda i: (0, i))
    shape = jax.ShapeDtypeStruct((rows, n), F32)
    gshape = jax.ShapeDtypeStruct((rows, n), GATE_DTYPE)
    return pl.pallas_call(
        _peersel_kernel,
        grid=(n // tn,),
        in_specs=[
            pl.BlockSpec((dq, d), lambda i: (0, 0), pipeline_mode=pl.Buffered(1)),
            pl.BlockSpec(keys.shape, lambda i: (0, 0, 0, 0), pipeline_mode=pl.Buffered(1)),
            pl.BlockSpec((d, tn), lambda i: (0, i)),
        ],
        out_specs=[out, out, out, out],
        out_shape=[gshape, gshape, shape, shape],
        compiler_params=_params("parallel"),
        name="peersel",
    )(w_qt, keys, xnt)


def _gelu2(x):
    k0 = math.sqrt(2.0 / math.pi)
    return x + x * jnp.tanh(x * (k0 + (k0 * 0.044715) * (x * x)))


def _peerffn_kernel(u_ref, vt_ref, xnt_ref, r2_ref, e2_ref, n1_ref, cf_ref, o_ref):
    e = pl.program_id(1)
    ec = u_ref.shape[0]
    nsub = ec // PEER_SUB
    per = PEER_SUB // PEER_KEYS
    tn = xnt_ref.shape[1]
    gdt = r2_ref.dtype
    zero = jnp.zeros((), gdt)
    rows = SUBLANES * (4 // jnp.dtype(gdt).itemsize)
    tile = (rows, tn)
    tiled = (PEER_KEYS // rows, rows, tn)

    def coefs(step, sub):
        parts = []
        for ii in range(per):
            k = sub * per + ii
            coef = jnp.zeros(tiled, gdt)
            for h in range(PEER_HEADS):
                hs = slice(h * PEER_KEYS, (h + 1) * PEER_KEYS)
                base = pl.multiple_of(h * PEER_KEYS + step * (nsub * per) + (k // SUBLANES) * SUBLANES,
                                      SUBLANES)
                ks = slice(k % SUBLANES, k % SUBLANES + 1)
                n1 = jnp.broadcast_to(n1_ref[pl.ds(base, SUBLANES), :][ks], tile).astype(gdt)
                cf = jnp.broadcast_to(cf_ref[pl.ds(base, SUBLANES), :][ks], tile).astype(gdt)
                r2 = r2_ref[hs, :].reshape(tiled)
                e2 = e2_ref[hs, :].reshape(tiled)
                coef = coef + jnp.where(r2 < n1[None], e2, zero) * cf[None]
            parts.append(coef.reshape(PEER_KEYS, tn))
        return jnp.concatenate(parts, axis=0).astype(F32)

    @pl.when(e == 0)
    def _():
        o_ref[...] = jnp.zeros(o_ref.shape, F32)

    def acts(sub):
        return jnp.dot(u_ref[sub * PEER_SUB:(sub + 1) * PEER_SUB, :], xnt_ref[...],
                       preferred_element_type=F32)

    coef, act = coefs(e, 0), acts(0)
    for sub in range(nsub):
        if sub + 1 < nsub:
            coef_next, act_next = coefs(e, sub + 1), acts(sub + 1)
        pt = (coef * _gelu2(act)).astype(vt_ref.dtype)
        o_ref[...] += jnp.dot(vt_ref[:, sub * PEER_SUB:(sub + 1) * PEER_SUB], pt,
                              preferred_element_type=F32)
        if sub + 1 < nsub:
            act, coef = act_next, coef_next


def _peerffn(u_tab, v_tab_t, xnt, r2, e2, n1, cf, tn, ec):
    ne, d = u_tab.shape
    n = xnt.shape[1]
    rows = PEER_HEADS * PEER_KEYS
    assert ec % (SUBLANES * PEER_KEYS) == 0 and ec % PEER_SUB == 0, ec
    sel = pl.BlockSpec((rows, tn), lambda i, e: (0, i))
    return pl.pallas_call(
        _peerffn_kernel,
        grid=(n // tn, ne // ec),
        in_specs=[
            pl.BlockSpec((ec, d), lambda i, e: (e, 0)),
            pl.BlockSpec((d, ec), lambda i, e: (0, e)),
            pl.BlockSpec((d, tn), lambda i, e: (0, i)),
            sel, sel, sel, sel,
        ],
        out_specs=pl.BlockSpec((d, tn), lambda i, e: (0, i)),
        out_shape=jax.ShapeDtypeStruct((d, n), F32),
        compiler_params=_params("parallel", "arbitrary"),
        name="peerffn",
    )(u_tab, v_tab_t, xnt, r2, e2, n1, cf)


def _final_kernel(x1_ref, pt_ref, g_ref, o_ref):
    o_ref[...] = _rms(x1_ref[...] + pt_ref[...].T, g_ref[...])


def _final(x1, peer_t, g, tm):
    n, d = x1.shape
    return pl.pallas_call(
        _final_kernel,
        grid=(n // tm,),
        in_specs=[
            pl.BlockSpec((tm, d), lambda i: (i, 0)),
            pl.BlockSpec((d, tm), lambda i: (0, i)),
            pl.BlockSpec((1, d), lambda i: (0, 0)),
        ],
        out_specs=pl.BlockSpec((tm, d), lambda i: (i, 0)),
        out_shape=jax.ShapeDtypeStruct((n, d), F32),
        compiler_params=_params("parallel"),
        name="final",
    )(x1, peer_t, g)


def _rope_tables(seq):
    half = ROPE_DIMS // 2
    dim = jnp.arange(LANES) % QK_DIM
    inv_freq = ROPE_THETA ** (-(2 * (dim % half)).astype(F32) / ROPE_DIMS)
    ang = jnp.arange(seq, dtype=F32)[:, None] * inv_freq[None, :]
    cos, sin = jnp.cos(ang), jnp.sin(ang)
    c = jnp.where(dim < ROPE_DIMS, cos, 1.0)
    sa = jnp.where(dim < half, -sin, 0.0)
    sb = jnp.where((dim >= half) & (dim < ROPE_DIMS), sin, 0.0)
    return c, sa, sb


def _tile(n, want):
    t = min(n, want)
    assert n % t == 0, (n, t)
    return t


def kernel(x, mix_norm_g, w_in, lambda_q1, lambda_k1, lambda_q2, lambda_k2, subln_g,
           gmlp_z_norm_g, gmlp_w_s, gmlp_b_s, gmlp_out_g, w_out, ffn_norm_g,
           peer_w_q, peer_sub_keys, peer_u, peer_v, final_norm_g):
    b, s, d = x.shape
    n = b * s
    depth = w_in.shape[0]
    cos_t, sa_t, sb_t = _rope_tables(s)
    tm_in = _tile(s, 512)
    tk = _tile(s, ATTN_KEY_TILE)
    tq = _tile(s, 2 * tk)
    tm = _tile(n, 512)
    tn = _tile(n, 512)
    ec = 1024

    x2 = x.reshape(n, d)
    for l in range(depth):
        lambda_init = 0.8 - 0.6 * math.exp(-0.3 * l)
        lam = (jnp.exp(jnp.sum(lambda_q1[l] * lambda_k1[l]))
               - jnp.exp(jnp.sum(lambda_q2[l] * lambda_k2[l])) + lambda_init).reshape(1).astype(F32)

        proj = _inproj(x2, mix_norm_g[l][None], w_in[l].astype(MXU_DTYPE), cos_t, sa_t, sb_t,
                       gmlp_z_norm_g[l][None], s, tm_in)
        attn = _attn(lam, proj, subln_g[l][None], 1.0 - lambda_init, b, tq, tk)
        b_full = jnp.broadcast_to(gmlp_b_s[l][:, :, None], (GROUPS, GMLP_LEN, GROUP_CH))
        gm = _gmlp(proj, gmlp_w_s[l], b_full, gmlp_out_g[l][None], tm)
        x1, xnt = _outproj(attn, gm, x2, w_out[l].astype(MXU_DTYPE), ffn_norm_g[l][None], tm)

        r2, e2, n1, cf = _peersel(peer_w_q[l].T.astype(MXU_DTYPE),
                                  peer_sub_keys[l].astype(MXU_DTYPE), xnt, tn)
        peer_t = _peerffn(peer_u[l].astype(MXU_DTYPE), peer_v[l].T.astype(MXU_DTYPE), xnt,
                          r2, e2, n1, cf, tn, ec)
        if l + 1 < depth:
            x2 = x1 + peer_t.T
    return _final(x1, peer_t, final_norm_g[None], tm).reshape(b, s, d)
```

```python
import functools
import math

import jax
import jax.numpy as jnp
from jax import lax
from jax.experimental import pallas as pl
from jax.experimental.pallas import tpu as pltpu

F32 = jnp.float32
MXU_DTYPE = jnp.bfloat16
GATE_DTYPE = jnp.bfloat16

EPS = 1e-6
NEG_INF = -1e30
CHUNK = 64
ROPE_THETA = 500000.0

HEADS = 8
QK_DIM = 64
V_DIM = 128
ROPE_DIMS = QK_DIM // 4
GROUPS = 8
GROUP_CH = 128
GMLP_LEN = 128
WIDTH = 1024

PEER_HEADS = 8
PEER_KEYS = 128
PEER_TOPK = 16

LANES = 128
SUBLANES = 8
MXU_COLS = 256
ATTN_KEY_TILE = 2 * MXU_COLS
Q_SCALE = QK_DIM ** -0.5 * math.log2(math.e)
PEER_SUB = 2 * MXU_COLS
PEER_ROWS = 512
VMEM_LIMIT = 56 * 1024 * 1024


def _params(*sem):
    return pltpu.CompilerParams(dimension_semantics=sem, vmem_limit_bytes=VMEM_LIMIT)


def _rms(x, g):
    return x * lax.rsqrt(jnp.mean(x * x, axis=-1, keepdims=True) + EPS) * g


def _inproj_kernel(x_ref, g_ref, w_ref, cos_ref, sa_ref, sb_ref, zg_ref, o_ref):
    hn = _rms(x_ref[...], g_ref[...]).astype(w_ref.dtype)
    c, sa, sb = cos_ref[...], sa_ref[...], sb_ref[...]
    zg = zg_ref[...]
    for j in (0, 1, 3, 4, 2):
        acc = jnp.dot(hn, w_ref[:, j * WIDTH:(j + 1) * WIDTH], preferred_element_type=F32)
        for h in range(HEADS):
            t = acc[:, h * LANES:(h + 1) * LANES]
            if j < 2:
                t = t * c + pltpu.roll(t, LANES - 8, 1) * sa + pltpu.roll(t, 8, 1) * sb
                if j == 0:
                    t = t * Q_SCALE
            elif j >= 3:
                t = jax.nn.gelu(t)
                if j == 4:
                    t = _rms(t, zg[:, h * LANES:(h + 1) * LANES])
            o_ref[j * HEADS + h] = t.astype(o_ref.dtype)


def _inproj(x2, g, w, cos_t, sa_t, sb_t, zg, seq, tm):
    n, d = x2.shape
    nseq = seq // tm
    tab = pl.BlockSpec((tm, LANES), lambda i: (i % nseq, 0))
    return pl.pallas_call(
        _inproj_kernel,
        grid=(n // tm,),
        in_specs=[
            pl.BlockSpec((tm, d), lambda i: (i, 0)),
            pl.BlockSpec((1, d), lambda i: (0, 0)),
            pl.BlockSpec((d, 5 * WIDTH), lambda i: (0, 0), pipeline_mode=pl.Buffered(1)),
            tab, tab, tab,
            pl.BlockSpec((1, WIDTH), lambda i: (0, 0)),
        ],
        out_specs=pl.BlockSpec((5 * HEADS, tm, LANES), lambda i: (0, i, 0)),
        out_shape=jax.ShapeDtypeStruct((5 * HEADS, n, LANES), MXU_DTYPE),
        compiler_params=_params("parallel"),
        name="inproj",
    )(x2, g, w, cos_t, sa_t, sb_t, zg)


def _attn_kernel(lam_ref, q_ref, k_ref, v_ref, g_ref, o_ref, vt_ref, qpad_ref, s_ref, mx_ref, m_ref,
                 l_ref, acc_ref, *, tq, tk, post_scale):
    qi = pl.program_id(2)
    nkb = vt_ref.shape[0]
    nsub = tq // tk
    chains = range(2 * nsub)

    @pl.when(qi == 0)
    def _():
        def tr(c, carry):
            off = pl.multiple_of(c * tk, tk)
            vt_ref[c] = v_ref[pl.ds(off, tk), :].astype(F32).T.astype(vt_ref.dtype)
            return carry
        lax.fori_loop(0, nkb, tr, 0)

    qt = q_ref[...].astype(F32).T
    row = lax.broadcasted_iota(jnp.int32, qt.shape, 0)
    qpad_ref[...] = jnp.concatenate([jnp.where(row < QK_DIM, qt, 0.0),
                                     jnp.where(row >= QK_DIM, qt, 0.0)], axis=1).astype(qpad_ref.dtype)

    m_ref[...] = jnp.full(m_ref.shape, NEG_INF, F32)
    l_ref[...] = jnp.zeros(l_ref.shape, F32)
    acc_ref[...] = jnp.zeros(acc_ref.shape, F32)

    def cols(c):
        return slice(c * tk, (c + 1) * tk)

    def scores(kb, c):
        return jnp.dot(kb, qpad_ref[:, cols(c)], preferred_element_type=F32)

    def put(c, s):
        s_ref[:, cols(c)] = s
        mx_ref[:, cols(c)] = jnp.max(s, axis=0, keepdims=True)

    def consume(s, mx, vb, c, masked):
        cs = cols(c)
        if masked:
            kc = lax.broadcasted_iota(jnp.int32, s.shape, 0) // CHUNK
            qc = lax.broadcasted_iota(jnp.int32, s.shape, 1) // CHUNK
            s = jnp.where(kc <= qc, s, NEG_INF)
            mx = jnp.max(s, axis=0, keepdims=True)
        m_old = m_ref[:, cs]
        m_new = jnp.maximum(m_old, mx)
        alpha = jnp.exp2(m_old - m_new)
        p = jnp.exp2(s - m_new)
        l_ref[:, cs] = alpha * l_ref[:, cs] + jnp.sum(p, axis=0, keepdims=True)
        acc_ref[:, cs] = alpha * acc_ref[:, cs] + jnp.dot(
            vb, p.astype(vb.dtype), preferred_element_type=F32)
        m_ref[:, cs] = m_new

    kb0 = k_ref[pl.ds(0, tk), :]
    for c in chains:
        put(c, scores(kb0, c))

    def body(j, carry):
        off = pl.multiple_of((j + 1) * tk, tk)
        kb = k_ref[pl.ds(off, tk), :]
        vb = vt_ref[j]
        for c in chains:
            s_cur, mx_cur = s_ref[:, cols(c)], mx_ref[:, cols(c)]
            s_next = scores(kb, c)
            consume(s_cur, mx_cur, vb, c, False)
            put(c, s_next)
        return carry
    nfull = qi * nsub
    lax.fori_loop(0, nfull, body, 0)

    for d in range(nsub):
        vb = vt_ref[nfull + d]
        if d + 1 < nsub:
            off = pl.multiple_of((nfull + d + 1) * tk, tk)
            kb = k_ref[pl.ds(off, tk), :]
        for c in chains:
            u = c % nsub
            if u < d:
                continue
            s_cur, mx_cur = s_ref[:, cols(c)], mx_ref[:, cols(c)]
            if u > d:
                s_next = scores(kb, c)
            consume(s_cur, mx_cur, vb, c, u == d)
            if u > d:
                put(c, s_next)

    inv = 1.0 / l_ref[...]
    o = acc_ref[...] * inv
    ot = o[:, :tq] - lam_ref[0] * o[:, tq:]
    o_ref[...] = (_rms(ot.T, g_ref[...]) * post_scale).astype(o_ref.dtype)


def _attn(lam, proj, subln_g, post_scale, batch, tq, tk):
    n = proj.shape[1]
    s = n // batch
    nq = s // tq
    kern = functools.partial(_attn_kernel, tq=tq, tk=tk, post_scale=post_scale)
    return pl.pallas_call(
        kern,
        grid=(batch, HEADS, nq),
        in_specs=[
            pl.BlockSpec(memory_space=pltpu.SMEM),
            pl.BlockSpec((None, tq, LANES), lambda bi, h, qi: (h, bi * nq + qi, 0)),
            pl.BlockSpec((None, s, LANES), lambda bi, h, qi: (HEADS + h, bi, 0)),
            pl.BlockSpec((None, s, LANES), lambda bi, h, qi: (2 * HEADS + h, bi, 0)),
            pl.BlockSpec((1, V_DIM), lambda bi, h, qi: (0, 0)),
        ],
        out_specs=pl.BlockSpec((tq, LANES), lambda bi, h, qi: (bi * nq + qi, h)),
        out_shape=jax.ShapeDtypeStruct((n, WIDTH), MXU_DTYPE),
        scratch_shapes=[
            pltpu.VMEM((s // tk, V_DIM, tk), MXU_DTYPE),
            pltpu.VMEM((2 * QK_DIM, 2 * tq), MXU_DTYPE),
            pltpu.VMEM((tk, 2 * tq), F32),
            pltpu.VMEM((1, 2 * tq), F32),
            pltpu.VMEM((1, 2 * tq), F32),
            pltpu.VMEM((1, 2 * tq), F32),
            pltpu.VMEM((V_DIM, 2 * tq), F32),
        ],
        compiler_params=_params("parallel", "parallel", "arbitrary"),
        name="attn",
    )(lam, proj, proj, proj, subln_g)


def _gmlp_kernel(u_ref, z_ref, w_ref, b_ref, g_ref, o_ref, gm_ref):
    tm = u_ref.shape[1]
    pc = lax.broadcasted_iota(jnp.int32, (GMLP_LEN, GMLP_LEN), 0) // CHUNK
    qc = lax.broadcasted_iota(jnp.int32, (GMLP_LEN, GMLP_LEN), 1) // CHUNK
    for g in range(GROUPS):
        w = jnp.where(pc >= qc, w_ref[g], 0.0).astype(z_ref.dtype)
        bias = b_ref[g]
        cs = slice(g * GROUP_CH, (g + 1) * GROUP_CH)
        for nb in range(tm // GMLP_LEN):
            rs = slice(nb * GMLP_LEN, (nb + 1) * GMLP_LEN)
            sz = jnp.dot(w, z_ref[g, rs, :], preferred_element_type=F32) + bias
            gm_ref[rs, cs] = u_ref[g, rs, :].astype(F32) * sz
    o_ref[...] = _rms(gm_ref[...], g_ref[...]).astype(o_ref.dtype)


def _gmlp(proj, w_s, b_full, out_g, tm):
    n = proj.shape[1]
    return pl.pallas_call(
        _gmlp_kernel,
        grid=(n // tm,),
        in_specs=[
            pl.BlockSpec((GROUPS, tm, GROUP_CH), lambda i: (3, i, 0)),
            pl.BlockSpec((GROUPS, tm, GROUP_CH), lambda i: (4, i, 0)),
            pl.BlockSpec((GROUPS, GMLP_LEN, GMLP_LEN), lambda i: (0, 0, 0)),
            pl.BlockSpec((GROUPS, GMLP_LEN, GROUP_CH), lambda i: (0, 0, 0)),
            pl.BlockSpec((1, WIDTH), lambda i: (0, 0)),
        ],
        out_specs=pl.BlockSpec((tm, WIDTH), lambda i: (i, 0)),
        out_shape=jax.ShapeDtypeStruct((n, WIDTH), MXU_DTYPE),
        scratch_shapes=[pltpu.VMEM((tm, WIDTH), F32)],
        compiler_params=_params("parallel"),
        name="gmlp",
    )(proj, proj, w_s, b_full, out_g)


def _outproj_kernel(a_ref, gm_ref, x_ref, w_ref, g_ref, x1_ref, xnt_ref):
    mixed = (jnp.dot(a_ref[...], w_ref[:WIDTH, :], preferred_element_type=F32)
             + jnp.dot(gm_ref[...], w_ref[WIDTH:, :], preferred_element_type=F32))
    x1 = x_ref[...] + mixed
    x1_ref[...] = x1
    xnt_ref[...] = _rms(x1, g_ref[...]).T.astype(xnt_ref.dtype)


def _outproj(attn2, gm, x2, w_out, ffn_g, tm):
    n, d = x2.shape
    return pl.pallas_call(
        _outproj_kernel,
        grid=(n // tm,),
        in_specs=[
            pl.BlockSpec((tm, WIDTH), lambda i: (i, 0)),
            pl.BlockSpec((tm, WIDTH), lambda i: (i, 0)),
            pl.BlockSpec((tm, d), lambda i: (i, 0)),
            pl.BlockSpec((2 * WIDTH, d), lambda i: (0, 0)),
            pl.BlockSpec((1, d), lambda i: (0, 0)),
        ],
        out_specs=[
            pl.BlockSpec((tm, d), lambda i: (i, 0)),
            pl.BlockSpec((d, tm), lambda i: (0, i)),
        ],
        out_shape=[
            jax.ShapeDtypeStruct((n, d), F32),
            jax.ShapeDtypeStruct((d, n), MXU_DTYPE),
        ],
        compiler_params=_params("parallel"),
        name="outproj",
    )(attn2, gm, x2, w_out, ffn_g)


def _sort_desc(v):
    v, n, k = list(v), len(v), 2
    while k <= n:
        j = k // 2
        while j >= 1:
            for i in range(n):
                p = i ^ j
                if p > i:
                    hi, lo = jnp.maximum(v[i], v[p]), jnp.minimum(v[i], v[p])
                    v[i], v[p] = (hi, lo) if (i & k) == 0 else (lo, hi)
            j //= 2
        k *= 2
    return v


def _merge_top(lists, singles, count):
    lists, singles, rows = list(lists), list(singles), []
    for r in range(count):
        head = lists[0]
        for s in singles:
            head = jnp.maximum(head, s) if s.shape == head.shape else head
        m = jnp.max(head, axis=0, keepdims=True)
        for s in singles:
            if s.shape != head.shape:
                m = jnp.maximum(m, s)
        rows.append(m)
        if r + 1 == count:
            break
        hit = lists[0] == m
        for k in range(min(len(lists), count - r - 1)):
            nxt = lists[k + 1] if k + 1 < len(lists) else -jnp.inf
            lists[k] = jnp.where(hit, nxt, lists[k])
        singles = [jnp.where(s == m, -jnp.inf, s) for s in singles]
    return rows


def _vregs(x):
    return [x[k * SUBLANES:(k + 1) * SUBLANES, :] for k in range(x.shape[0] // SUBLANES)]


def _peersel_kernel(w_ref, k_ref, xnt_ref, r2_ref, e2_ref, n1_ref, cf_ref):
    nk = PEER_KEYS

    def scores(h):
        q = jnp.dot(w_ref[h * 2 * nk:(h + 1) * 2 * nk, :], xnt_ref[...],
                    preferred_element_type=F32).astype(xnt_ref.dtype)
        return (jnp.dot(k_ref[h, 0], q[:nk], preferred_element_type=F32),
                jnp.dot(k_ref[h, 1], q[nk:], preferred_element_type=F32))

    s = scores(0)
    for h in range(PEER_HEADS):
        if h + 1 < PEER_HEADS:
            s_next = scores(h + 1)
        rows = slice(h * nk, (h + 1) * nk)
        for c in range(xnt_ref.shape[1] // LANES):
            cs = slice(c * LANES, (c + 1) * LANES)
            r2, e2, n1, cf = _peersel_column(s[0][:, cs], s[1][:, cs])
            r2_ref[rows, cs] = r2.astype(r2_ref.dtype)
            e2_ref[rows, cs] = e2.astype(e2_ref.dtype)
            n1_ref[rows, cs] = n1
            cf_ref[rows, cs] = cf
        if h + 1 < PEER_HEADS:
            s = s_next


def _peersel_column(s1, s2):
    top = PEER_TOPK + 1
    a1 = _merge_top(_sort_desc(_vregs(s1)), [], top)
    a2 = _merge_top(_sort_desc(_vregs(s2)), [], top)

    a1lo = jnp.concatenate(a1[:SUBLANES], axis=0)
    sub = lax.broadcasted_iota(jnp.int32, a1lo.shape, 0)
    lists = [jnp.where(sub < min(SUBLANES, top // (c + 1)), a1lo + a2[c], -jnp.inf)
             for c in range(top)]
    singles = [jnp.concatenate(a1[SUBLANES:2 * SUBLANES], axis=0) + a2[0], a1[PEER_TOPK] + a2[0]]
    best = _merge_top(lists, singles, top)

    zsum = jnp.zeros_like(best[0])
    for kk in range(PEER_TOPK):
        zsum = zsum + jnp.exp(best[kk] - best[0])
    tau = 0.5 * (best[PEER_TOPK - 1] + best[PEER_TOPK])
    rank2 = jnp.full(s2.shape, PEER_KEYS - 1, F32)
    for c in reversed(range(top)):
        rank2 = jnp.where(s2 >= a2[c], float(c), rank2)
    thr = tau - s1
    n1 = jnp.zeros_like(s1)
    for c in range(SUBLANES):
        n1 = jnp.where(a2[c] >= thr, float(c + 1), n1)
    thr_top = tau - a1[0]
    n_top = jnp.zeros_like(thr_top)
    for c in range(SUBLANES, top):
        n_top = jnp.where(a2[c] >= thr_top, float(c + 1), n_top)
    n1 = jnp.where(s1 == a1[0], jnp.maximum(n1, n_top), n1)
    return rank2, jnp.exp(s2 - a2[0]), n1, jnp.exp(s1 - a1[0]) * (0.5 / zsum)


def _peersel(w_qt, keys, xnt, tn):
    dq, d = w_qt.shape
    n = xnt.shape[1]
    rows = PEER_HEADS * PEER_KEYS
    out = pl.BlockSpec((rows, tn), lambda i: (0, i))
    shape = jax.ShapeDtypeStruct((rows, n), F32)
    gshape = jax.ShapeDtypeStruct((rows, n), GATE_DTYPE)
    return pl.pallas_call(
        _peersel_kernel,
        grid=(n // tn,),
        in_specs=[
            pl.BlockSpec((dq, d), lambda i: (0, 0), pipeline_mode=pl.Buffered(1)),
            pl.BlockSpec(keys.shape, lambda i: (0, 0, 0, 0), pipeline_mode=pl.Buffered(1)),
            pl.BlockSpec((d, tn), lambda i: (0, i)),
        ],
        out_specs=[out, out, out, out],
        out_shape=[gshape, gshape, shape, shape],
        compiler_params=_params("parallel"),
        name="peersel",
    )(w_qt, keys, xnt)


def _gelu2(x):
    k0 = math.sqrt(2.0 / math.pi)
    return x + x * jnp.tanh(x * (k0 + (k0 * 0.044715) * (x * x)))


def _peerffn_kernel(u_ref, vt_ref, xnt_ref, r2_ref, e2_ref, n1_ref, cf_ref, o_ref):
    e = pl.program_id(1)
    ec = u_ref.shape[0]
    nsub = ec // PEER_SUB
    per = PEER_SUB // PEER_KEYS
    tn = xnt_ref.shape[1]
    gdt = r2_ref.dtype
    zero = jnp.zeros((), gdt)
    rows = SUBLANES * (4 // jnp.dtype(gdt).itemsize)
    tile = (rows, tn)
    tiled = (PEER_KEYS // rows, rows, tn)

    def coefs(step, sub):
        parts = []
        for ii in range(per):
            k = sub * per + ii
            coef = jnp.zeros(tiled, gdt)
            for h in range(PEER_HEADS):
                hs = slice(h * PEER_KEYS, (h + 1) * PEER_KEYS)
                base = pl.multiple_of(h * PEER_KEYS + step * (nsub * per) + (k // SUBLANES) * SUBLANES,
                                      SUBLANES)
                ks = slice(k % SUBLANES, k % SUBLANES + 1)
                n1 = jnp.broadcast_to(n1_ref[pl.ds(base, SUBLANES), :][ks], tile).astype(gdt)
                cf = jnp.broadcast_to(cf_ref[pl.ds(base, SUBLANES), :][ks], tile).astype(gdt)
                r2 = r2_ref[hs, :].reshape(tiled)
                e2 = e2_ref[hs, :].reshape(tiled)
                coef = coef + jnp.where(r2 < n1[None], e2, zero) * cf[None]
            parts.append(coef.reshape(PEER_KEYS, tn))
        return jnp.concatenate(parts, axis=0).astype(F32)

    @pl.when(e == 0)
    def _():
        o_ref[...] = jnp.zeros(o_ref.shape, F32)

    def acts(sub):
        return jnp.dot(u_ref[sub * PEER_SUB:(sub + 1) * PEER_SUB, :], xnt_ref[...],
                       preferred_element_type=F32)

    coef, act = coefs(e, 0), acts(0)
    for sub in range(nsub):
        if sub + 1 < nsub:
            coef_next, act_next = coefs(e, sub + 1), acts(sub + 1)
        pt = (coef * _gelu2(act)).astype(vt_ref.dtype)
        for r in range(0, o_ref.shape[0], PEER_ROWS):
            o_ref[r:r + PEER_ROWS, :] += jnp.dot(
                vt_ref[r:r + PEER_ROWS, sub * PEER_SUB:(sub + 1) * PEER_SUB], pt,
                preferred_element_type=F32)
        if sub + 1 < nsub:
            act, coef = act_next, coef_next


def _peerffn(u_tab, v_tab_t, xnt, r2, e2, n1, cf, tn, ec):
    ne, d = u_tab.shape
    n = xnt.shape[1]
    rows = PEER_HEADS * PEER_KEYS
    assert ec % (SUBLANES * PEER_KEYS) == 0 and ec % PEER_SUB == 0, ec
    sel = pl.BlockSpec((rows, tn), lambda i, e: (0, i))
    return pl.pallas_call(
        _peerffn_kernel,
        grid=(n // tn, ne // ec),
        in_specs=[
            pl.BlockSpec((ec, d), lambda i, e: (e, 0)),
            pl.BlockSpec((d, ec), lambda i, e: (0, e)),
            pl.BlockSpec((d, tn), lambda i, e: (0, i)),
            sel, sel, sel, sel,
        ],
        out_specs=pl.BlockSpec((d, tn), lambda i, e: (0, i)),
        out_shape=jax.ShapeDtypeStruct((d, n), F32),
        compiler_params=_params("parallel", "arbitrary"),
        name="peerffn",
    )(u_tab, v_tab_t, xnt, r2, e2, n1, cf)


def _final_kernel(x1_ref, pt_ref, g_ref, o_ref):
    o_ref[...] = _rms(x1_ref[...] + pt_ref[...].T, g_ref[...])


def _final(x1, peer_t, g, tm):
    n, d = x1.shape
    return pl.pallas_call(
        _final_kernel,
        grid=(n // tm,),
        in_specs=[
            pl.BlockSpec((tm, d), lambda i: (i, 0)),
            pl.BlockSpec((d, tm), lambda i: (0, i)),
            pl.BlockSpec((1, d), lambda i: (0, 0)),
        ],
        out_specs=pl.BlockSpec((tm, d), lambda i: (i, 0)),
        out_shape=jax.ShapeDtypeStruct((n, d), F32),
        compiler_params=_params("parallel"),
        name="final",
    )(x1, peer_t, g)


def _rope_tables(seq):
    half = ROPE_DIMS // 2
    dim = jnp.arange(LANES) % QK_DIM
    inv_freq = ROPE_THETA ** (-(2 * (dim % half)).astype(F32) / ROPE_DIMS)
    ang = jnp.arange(seq, dtype=F32)[:, None] * inv_freq[None, :]
    cos, sin = jnp.cos(ang), jnp.sin(ang)
    c = jnp.where(dim < ROPE_DIMS, cos, 1.0)
    sa = jnp.where(dim < half, -sin, 0.0)
    sb = jnp.where((dim >= half) & (dim < ROPE_DIMS), sin, 0.0)
    return c, sa, sb


def _tile(n, want):
    t = min(n, want)
    assert n % t == 0, (n, t)
    return t


def kernel(x, mix_norm_g, w_in, lambda_q1, lambda_k1, lambda_q2, lambda_k2, subln_g,
           gmlp_z_norm_g, gmlp_w_s, gmlp_b_s, gmlp_out_g, w_out, ffn_norm_g,
           peer_w_q, peer_sub_keys, peer_u, peer_v, final_norm_g):
    b, s, d = x.shape
    n = b * s
    depth = w_in.shape[0]
    cos_t, sa_t, sb_t = _rope_tables(s)
    tm_in = _tile(s, 512)
    tk = _tile(s, ATTN_KEY_TILE)
    tq = _tile(s, 2 * tk)
    tm = _tile(n, 512)
    tn = _tile(n, 512)
    ec = 1024

    x2 = x.reshape(n, d)
    for l in range(depth):
        lambda_init = 0.8 - 0.6 * math.exp(-0.3 * l)
        lam = (jnp.exp(jnp.sum(lambda_q1[l] * lambda_k1[l]))
               - jnp.exp(jnp.sum(lambda_q2[l] * lambda_k2[l])) + lambda_init).reshape(1).astype(F32)

        proj = _inproj(x2, mix_norm_g[l][None], w_in[l].astype(MXU_DTYPE), cos_t, sa_t, sb_t,
                       gmlp_z_norm_g[l][None], s, tm_in)
        attn = _attn(lam, proj, subln_g[l][None], 1.0 - lambda_init, b, tq, tk)
        b_full = jnp.broadcast_to(gmlp_b_s[l][:, :, None], (GROUPS, GMLP_LEN, GROUP_CH))
        gm = _gmlp(proj, gmlp_w_s[l], b_full, gmlp_out_g[l][None], tm)
        x1, xnt = _outproj(attn, gm, x2, w_out[l].astype(MXU_DTYPE), ffn_norm_g[l][None], tm)

        r2, e2, n1, cf = _peersel(peer_w_q[l].T.astype(MXU_DTYPE),
                                  peer_sub_keys[l].astype(MXU_DTYPE), xnt, tn)
        peer_t = _peerffn(peer_u[l].astype(MXU_DTYPE), peer_v[l].T.astype(MXU_DTYPE), xnt,
                          r2, e2, n1, cf, tn, ec)
        if l + 1 < depth:
            x2 = x1 + peer_t.T
    return _final(x1, peer_t, final_norm_g[None], tm).reshape(b, s, d)
```

```python
import functools
import math

import jax
import jax.numpy as jnp
from jax import lax
from jax.experimental import pallas as pl
from jax.experimental.pallas import tpu as pltpu

F32 = jnp.float32
MXU_DTYPE = jnp.bfloat16
GATE_DTYPE = jnp.bfloat16

EPS = 1e-6
NEG_INF = -1e30
CHUNK = 64
ROPE_THETA = 500000.0

HEADS = 8
QK_DIM = 64
V_DIM = 128
ROPE_DIMS = QK_DIM // 4
GROUPS = 8
GROUP_CH = 128
GMLP_LEN = 128
WIDTH = 1024

PEER_HEADS = 8
PEER_KEYS = 128
PEER_TOPK = 16

LANES = 128
SUBLANES = 8
MXU_COLS = 256
ATTN_KEY_TILE = 2 * MXU_COLS
Q_SCALE = QK_DIM ** -0.5 * math.log2(math.e)
PEER_SUB = 2 * MXU_COLS
PEER_ROWS = 512
PEER_GATE = MXU_COLS
VMEM_LIMIT = 56 * 1024 * 1024


def _params(*sem):
    return pltpu.CompilerParams(dimension_semantics=sem, vmem_limit_bytes=VMEM_LIMIT)


def _rms(x, g):
    return x * lax.rsqrt(jnp.mean(x * x, axis=-1, keepdims=True) + EPS) * g


def _inproj_kernel(x_ref, g_ref, w_ref, cos_ref, sa_ref, sb_ref, zg_ref, o_ref):
    hn = _rms(x_ref[...], g_ref[...]).astype(w_ref.dtype)
    c, sa, sb = cos_ref[...], sa_ref[...], sb_ref[...]
    zg = zg_ref[...]
    for j in (0, 1, 3, 4, 2):
        acc = jnp.dot(hn, w_ref[:, j * WIDTH:(j + 1) * WIDTH], preferred_element_type=F32)
        for h in range(HEADS):
            t = acc[:, h * LANES:(h + 1) * LANES]
            if j < 2:
                t = t * c + pltpu.roll(t, LANES - 8, 1) * sa + pltpu.roll(t, 8, 1) * sb
                if j == 0:
                    t = t * Q_SCALE
            elif j >= 3:
                t = jax.nn.gelu(t)
                if j == 4:
                    t = _rms(t, zg[:, h * LANES:(h + 1) * LANES])
            o_ref[j * HEADS + h] = t.astype(o_ref.dtype)


def _inproj(x2, g, w, cos_t, sa_t, sb_t, zg, seq, tm):
    n, d = x2.shape
    nseq = seq // tm
    tab = pl.BlockSpec((tm, LANES), lambda i: (i % nseq, 0))
    return pl.pallas_call(
        _inproj_kernel,
        grid=(n // tm,),
        in_specs=[
            pl.BlockSpec((tm, d), lambda i: (i, 0)),
            pl.BlockSpec((1, d), lambda i: (0, 0)),
            pl.BlockSpec((d, 5 * WIDTH), lambda i: (0, 0), pipeline_mode=pl.Buffered(1)),
            tab, tab, tab,
            pl.BlockSpec((1, WIDTH), lambda i: (0, 0)),
        ],
        out_specs=pl.BlockSpec((5 * HEADS, tm, LANES), lambda i: (0, i, 0)),
        out_shape=jax.ShapeDtypeStruct((5 * HEADS, n, LANES), MXU_DTYPE),
        compiler_params=_params("parallel"),
        name="inproj",
    )(x2, g, w, cos_t, sa_t, sb_t, zg)


def _attn_kernel(lam_ref, q_ref, k_ref, v_ref, g_ref, o_ref, vt_ref, qpad_ref, s_ref, mx_ref, m_ref,
                 l_ref, acc_ref, *, tq, tk, post_scale):
    qi = pl.program_id(2)
    nkb = vt_ref.shape[0]
    nsub = tq // tk
    chains = range(2 * nsub)

    @pl.when(qi == 0)
    def _():
        def tr(c, carry):
            off = pl.multiple_of(c * tk, tk)
            vt_ref[c] = v_ref[pl.ds(off, tk), :].astype(F32).T.astype(vt_ref.dtype)
            return carry
        lax.fori_loop(0, nkb, tr, 0)

    qt = q_ref[...].astype(F32).T
    row = lax.broadcasted_iota(jnp.int32, qt.shape, 0)
    qpad_ref[...] = jnp.concatenate([jnp.where(row < QK_DIM, qt, 0.0),
                                     jnp.where(row >= QK_DIM, qt, 0.0)], axis=1).astype(qpad_ref.dtype)

    m_ref[...] = jnp.full(m_ref.shape, NEG_INF, F32)
    l_ref[...] = jnp.zeros(l_ref.shape, F32)
    acc_ref[...] = jnp.zeros(acc_ref.shape, F32)

    def cols(c):
        return slice(c * tk, (c + 1) * tk)

    def scores(kb, c):
        return jnp.dot(kb, qpad_ref[:, cols(c)], preferred_element_type=F32)

    def put(c, s):
        s_ref[:, cols(c)] = s
        mx_ref[:, cols(c)] = jnp.max(s, axis=0, keepdims=True)

    def consume(s, mx, vb, c, masked):
        cs = cols(c)
        if masked:
            kc = lax.broadcasted_iota(jnp.int32, s.shape, 0) // CHUNK
            qc = lax.broadcasted_iota(jnp.int32, s.shape, 1) // CHUNK
            s = jnp.where(kc <= qc, s, NEG_INF)
            mx = jnp.max(s, axis=0, keepdims=True)
        m_old = m_ref[:, cs]
        m_new = jnp.maximum(m_old, mx)
        alpha = jnp.exp2(m_old - m_new)
        p = jnp.exp2(s - m_new)
        l_ref[:, cs] = alpha * l_ref[:, cs] + jnp.sum(p, axis=0, keepdims=True)
        acc_ref[:, cs] = alpha * acc_ref[:, cs] + jnp.dot(
            vb, p.astype(vb.dtype), preferred_element_type=F32)
        m_ref[:, cs] = m_new

    kb0 = k_ref[pl.ds(0, tk), :]
    for c in chains:
        put(c, scores(kb0, c))

    def body(j, carry):
        off = pl.multiple_of((j + 1) * tk, tk)
        kb = k_ref[pl.ds(off, tk), :]
        vb = vt_ref[j]
        for c in chains:
            s_cur, mx_cur = s_ref[:, cols(c)], mx_ref[:, cols(c)]
            s_next = scores(kb, c)
            consume(s_cur, mx_cur, vb, c, False)
            put(c, s_next)
        return carry
    nfull = qi * nsub
    lax.fori_loop(0, nfull, body, 0)

    for d in range(nsub):
        vb = vt_ref[nfull + d]
        if d + 1 < nsub:
            off = pl.multiple_of((nfull + d + 1) * tk, tk)
            kb = k_ref[pl.ds(off, tk), :]
        for c in chains:
            u = c % nsub
            if u < d:
                continue
            s_cur, mx_cur = s_ref[:, cols(c)], mx_ref[:, cols(c)]
            if u > d:
                s_next = scores(kb, c)
            consume(s_cur, mx_cur, vb, c, u == d)
            if u > d:
                put(c, s_next)

    inv = 1.0 / l_ref[...]
    o = acc_ref[...] * inv
    ot = o[:, :tq] - lam_ref[0] * o[:, tq:]
    o_ref[...] = (_rms(ot.T, g_ref[...]) * post_scale).astype(o_ref.dtype)


def _attn(lam, proj, subln_g, post_scale, batch, tq, tk):
    n = proj.shape[1]
    s = n // batch
    nq = s // tq
    kern = functools.partial(_attn_kernel, tq=tq, tk=tk, post_scale=post_scale)
    return pl.pallas_call(
        kern,
        grid=(batch, HEADS, nq),
        in_specs=[
            pl.BlockSpec(memory_space=pltpu.SMEM),
            pl.BlockSpec((None, tq, LANES), lambda bi, h, qi: (h, bi * nq + qi, 0)),
            pl.BlockSpec((None, s, LANES), lambda bi, h, qi: (HEADS + h, bi, 0)),
            pl.BlockSpec((None, s, LANES), lambda bi, h, qi: (2 * HEADS + h, bi, 0)),
            pl.BlockSpec((1, V_DIM), lambda bi, h, qi: (0, 0)),
        ],
        out_specs=pl.BlockSpec((tq, LANES), lambda bi, h, qi: (bi * nq + qi, h)),
        out_shape=jax.ShapeDtypeStruct((n, WIDTH), MXU_DTYPE),
        scratch_shapes=[
            pltpu.VMEM((s // tk, V_DIM, tk), MXU_DTYPE),
            pltpu.VMEM((2 * QK_DIM, 2 * tq), MXU_DTYPE),
            pltpu.VMEM((tk, 2 * tq), F32),
            pltpu.VMEM((1, 2 * tq), F32),
            pltpu.VMEM((1, 2 * tq), F32),
            pltpu.VMEM((1, 2 * tq), F32),
            pltpu.VMEM((V_DIM, 2 * tq), F32),
        ],
        compiler_params=_params("parallel", "parallel", "arbitrary"),
        name="attn",
    )(lam, proj, proj, proj, subln_g)


def _gmlp_kernel(u_ref, z_ref, w_ref, b_ref, g_ref, o_ref, gm_ref):
    tm = u_ref.shape[1]
    pc = lax.broadcasted_iota(jnp.int32, (GMLP_LEN, GMLP_LEN), 0) // CHUNK
    qc = lax.broadcasted_iota(jnp.int32, (GMLP_LEN, GMLP_LEN), 1) // CHUNK
    for g in range(GROUPS):
        w = jnp.where(pc >= qc, w_ref[g], 0.0).astype(z_ref.dtype)
        bias = b_ref[g]
        cs = slice(g * GROUP_CH, (g + 1) * GROUP_CH)
        for nb in range(tm // GMLP_LEN):
            rs = slice(nb * GMLP_LEN, (nb + 1) * GMLP_LEN)
            sz = jnp.dot(w, z_ref[g, rs, :], preferred_element_type=F32) + bias
            gm_ref[rs, cs] = u_ref[g, rs, :].astype(F32) * sz
    o_ref[...] = _rms(gm_ref[...], g_ref[...]).astype(o_ref.dtype)


def _gmlp(proj, w_s, b_full, out_g, tm):
    n = proj.shape[1]
    return pl.pallas_call(
        _gmlp_kernel,
        grid=(n // tm,),
        in_specs=[
            pl.BlockSpec((GROUPS, tm, GROUP_CH), lambda i: (3, i, 0)),
            pl.BlockSpec((GROUPS, tm, GROUP_CH), lambda i: (4, i, 0)),
            pl.BlockSpec((GROUPS, GMLP_LEN, GMLP_LEN), lambda i: (0, 0, 0)),
            pl.BlockSpec((GROUPS, GMLP_LEN, GROUP_CH), lambda i: (0, 0, 0)),
            pl.BlockSpec((1, WIDTH), lambda i: (0, 0)),
        ],
        out_specs=pl.BlockSpec((tm, WIDTH), lambda i: (i, 0)),
        out_shape=jax.ShapeDtypeStruct((n, WIDTH), MXU_DTYPE),
        scratch_shapes=[pltpu.VMEM((tm, WIDTH), F32)],
        compiler_params=_params("parallel"),
        name="gmlp",
    )(proj, proj, w_s, b_full, out_g)


def _outproj_kernel(a_ref, gm_ref, x_ref, w_ref, g_ref, x1_ref, xnt_ref):
    mixed = (jnp.dot(a_ref[...], w_ref[:WIDTH, :], preferred_element_type=F32)
             + jnp.dot(gm_ref[...], w_ref[WIDTH:, :], preferred_element_type=F32))
    x1 = x_ref[...] + mixed
    x1_ref[...] = x1
    xnt_ref[...] = _rms(x1, g_ref[...]).T.astype(xnt_ref.dtype)


def _outproj(attn2, gm, x2, w_out, ffn_g, tm):
    n, d = x2.shape
    return pl.pallas_call(
        _outproj_kernel,
        grid=(n // tm,),
        in_specs=[
            pl.BlockSpec((tm, WIDTH), lambda i: (i, 0)),
            pl.BlockSpec((tm, WIDTH), lambda i: (i, 0)),
            pl.BlockSpec((tm, d), lambda i: (i, 0)),
            pl.BlockSpec((2 * WIDTH, d), lambda i: (0, 0)),
            pl.BlockSpec((1, d), lambda i: (0, 0)),
        ],
        out_specs=[
            pl.BlockSpec((tm, d), lambda i: (i, 0)),
            pl.BlockSpec((d, tm), lambda i: (0, i)),
        ],
        out_shape=[
            jax.ShapeDtypeStruct((n, d), F32),
            jax.ShapeDtypeStruct((d, n), MXU_DTYPE),
        ],
        compiler_params=_params("parallel"),
        name="outproj",
    )(attn2, gm, x2, w_out, ffn_g)


def _sort_desc(v):
    v, n, k = list(v), len(v), 2
    while k <= n:
        j = k // 2
        while j >= 1:
            for i in range(n):
                p = i ^ j
                if p > i:
                    hi, lo = jnp.maximum(v[i], v[p]), jnp.minimum(v[i], v[p])
                    v[i], v[p] = (hi, lo) if (i & k) == 0 else (lo, hi)
            j //= 2
        k *= 2
    return v


def _merge_top(lists, singles, count):
    lists, singles, rows = list(lists), list(singles), []
    for r in range(count):
        head = lists[0]
        for s in singles:
            head = jnp.maximum(head, s) if s.shape == head.shape else head
        m = jnp.max(head, axis=0, keepdims=True)
        for s in singles:
            if s.shape != head.shape:
                m = jnp.maximum(m, s)
        rows.append(m)
        if r + 1 == count:
            break
        hit = lists[0] == m
        for k in range(min(len(lists), count - r - 1)):
            nxt = lists[k + 1] if k + 1 < len(lists) else -jnp.inf
            lists[k] = jnp.where(hit, nxt, lists[k])
        singles = [jnp.where(s == m, -jnp.inf, s) for s in singles]
    return rows


def _vregs(x):
    return [x[k * SUBLANES:(k + 1) * SUBLANES, :] for k in range(x.shape[0] // SUBLANES)]


def _peersel_kernel(w_ref, k_ref, xnt_ref, r2_ref, e2_ref, n1_ref, cf_ref):
    nk = PEER_KEYS

    def scores(h):
        q = jnp.dot(w_ref[h * 2 * nk:(h + 1) * 2 * nk, :], xnt_ref[...],
                    preferred_element_type=F32).astype(xnt_ref.dtype)
        return (jnp.dot(k_ref[h, 0], q[:nk], preferred_element_type=F32),
                jnp.dot(k_ref[h, 1], q[nk:], preferred_element_type=F32))

    s = scores(0)
    for h in range(PEER_HEADS):
        if h + 1 < PEER_HEADS:
            s_next = scores(h + 1)
        rows = slice(h * nk, (h + 1) * nk)
        for c in range(xnt_ref.shape[1] // LANES):
            cs = slice(c * LANES, (c + 1) * LANES)
            r2, e2, n1, cf = _peersel_column(s[0][:, cs], s[1][:, cs])
            r2_ref[rows, cs] = r2.astype(r2_ref.dtype)
            e2_ref[rows, cs] = e2.astype(e2_ref.dtype)
            n1_ref[rows, cs] = n1
            cf_ref[rows, cs] = cf
        if h + 1 < PEER_HEADS:
            s = s_next


def _peersel_column(s1, s2):
    top = PEER_TOPK + 1
    a1 = _merge_top(_sort_desc(_vregs(s1)), [], top)
    a2 = _merge_top(_sort_desc(_vregs(s2)), [], top)

    a1lo = jnp.concatenate(a1[:SUBLANES], axis=0)
    sub = lax.broadcasted_iota(jnp.int32, a1lo.shape, 0)
    lists = [jnp.where(sub < min(SUBLANES, top // (c + 1)), a1lo + a2[c], -jnp.inf)
             for c in range(top)]
    singles = [jnp.concatenate(a1[SUBLANES:2 * SUBLANES], axis=0) + a2[0], a1[PEER_TOPK] + a2[0]]
    best = _merge_top(lists, singles, top)

    zsum = jnp.zeros_like(best[0])
    for kk in range(PEER_TOPK):
        zsum = zsum + jnp.exp(best[kk] - best[0])
    tau = 0.5 * (best[PEER_TOPK - 1] + best[PEER_TOPK])
    rank2 = jnp.full(s2.shape, PEER_KEYS - 1, F32)
    for c in reversed(range(top)):
        rank2 = jnp.where(s2 >= a2[c], float(c), rank2)
    thr = tau - s1
    n1 = jnp.zeros_like(s1)
    for c in range(SUBLANES):
        n1 = jnp.where(a2[c] >= thr, float(c + 1), n1)
    thr_top = tau - a1[0]
    n_top = jnp.zeros_like(thr_top)
    for c in range(SUBLANES, top):
        n_top = jnp.where(a2[c] >= thr_top, float(c + 1), n_top)
    n1 = jnp.where(s1 == a1[0], jnp.maximum(n1, n_top), n1)
    return rank2, jnp.exp(s2 - a2[0]), n1, jnp.exp(s1 - a1[0]) * (0.5 / zsum)


def _peersel(w_qt, keys, xnt, tn):
    dq, d = w_qt.shape
    n = xnt.shape[1]
    rows = PEER_HEADS * PEER_KEYS
    out = pl.BlockSpec((rows, tn), lambda i: (0, i))
    shape = jax.ShapeDtypeStruct((rows, n), F32)
    gshape = jax.ShapeDtypeStruct((rows, n), GATE_DTYPE)
    return pl.pallas_call(
        _peersel_kernel,
        grid=(n // tn,),
        in_specs=[
            pl.BlockSpec((dq, d), lambda i: (0, 0), pipeline_mode=pl.Buffered(1)),
            pl.BlockSpec(keys.shape, lambda i: (0, 0, 0, 0), pipeline_mode=pl.Buffered(1)),
            pl.BlockSpec((d, tn), lambda i: (0, i)),
        ],
        out_specs=[out, out, out, out],
        out_shape=[gshape, gshape, shape, shape],
        compiler_params=_params("parallel"),
        name="peersel",
    )(w_qt, keys, xnt)


def _gelu2(x):
    k0 = math.sqrt(2.0 / math.pi)
    return x + x * jnp.tanh(x * (k0 + (k0 * 0.044715) * (x * x)))


def _peerffn_kernel(u_ref, vt_ref, xnt_ref, r2_ref, e2_ref, n1_ref, cf_ref, o_ref):
    e = pl.program_id(1)
    ec = u_ref.shape[0]
    nsub = ec // PEER_SUB
    tn = xnt_ref.shape[1]
    gdt = r2_ref.dtype
    zero = jnp.zeros((), gdt)
    rows = SUBLANES * (4 // jnp.dtype(gdt).itemsize)
    tile = (rows, tn)
    tiled = (PEER_KEYS // rows, rows, tn)

    def coefs(step, g):
        parts = []
        for ii in range(PEER_GATE // PEER_KEYS):
            k = g * (PEER_GATE // PEER_KEYS) + ii
            coef = jnp.zeros(tiled, gdt)
            for h in range(PEER_HEADS):
                hs = slice(h * PEER_KEYS, (h + 1) * PEER_KEYS)
                base = pl.multiple_of(h * PEER_KEYS + step * (ec // PEER_KEYS)
                                      + (k // SUBLANES) * SUBLANES, SUBLANES)
                ks = slice(k % SUBLANES, k % SUBLANES + 1)
                n1 = jnp.broadcast_to(n1_ref[pl.ds(base, SUBLANES), :][ks], tile).astype(gdt)
                cf = jnp.broadcast_to(cf_ref[pl.ds(base, SUBLANES), :][ks], tile).astype(gdt)
                r2 = r2_ref[hs, :].reshape(tiled)
                e2 = e2_ref[hs, :].reshape(tiled)
                coef = coef + jnp.where(r2 < n1[None], e2, zero) * cf[None]
            parts.append(coef.reshape(PEER_KEYS, tn))
        return jnp.concatenate(parts, axis=0).astype(F32)

    def gated(sub):
        parts = []
        for g in range(sub * PEER_SUB // PEER_GATE, (sub + 1) * PEER_SUB // PEER_GATE):
            act = jnp.dot(u_ref[g * PEER_GATE:(g + 1) * PEER_GATE, :], xnt_ref[...],
                          preferred_element_type=F32)
            parts.append((coefs(e, g) * _gelu2(act)).astype(vt_ref.dtype))
        return jnp.concatenate(parts, axis=0)

    @pl.when(e == 0)
    def _():
        o_ref[...] = jnp.zeros(o_ref.shape, F32)

    pt = gated(0)
    for sub in range(nsub):
        if sub + 1 < nsub:
            pt_next = gated(sub + 1)
        for r in range(0, o_ref.shape[0], PEER_ROWS):
            o_ref[r:r + PEER_ROWS, :] += jnp.dot(
                vt_ref[r:r + PEER_ROWS, sub * PEER_SUB:(sub + 1) * PEER_SUB], pt,
                preferred_element_type=F32)
        if sub + 1 < nsub:
            pt = pt_next


def _peerffn(u_tab, v_tab_t, xnt, r2, e2, n1, cf, tn, ec):
    ne, d = u_tab.shape
    n = xnt.shape[1]
    rows = PEER_HEADS * PEER_KEYS
    assert ec % (SUBLANES * PEER_KEYS) == 0 and ec % PEER_SUB == 0, ec
    sel = pl.BlockSpec((rows, tn), lambda i, e: (0, i))
    return pl.pallas_call(
        _peerffn_kernel,
        grid=(n // tn, ne // ec),
        in_specs=[
            pl.BlockSpec((ec, d), lambda i, e: (e, 0)),
            pl.BlockSpec((d, ec), lambda i, e: (0, e)),
            pl.BlockSpec((d, tn), lambda i, e: (0, i)),
            sel, sel, sel, sel,
        ],
        out_specs=pl.BlockSpec((d, tn), lambda i, e: (0, i)),
        out_shape=jax.ShapeDtypeStruct((d, n), F32),
        compiler_params=_params("parallel", "arbitrary"),
        name="peerffn",
    )(u_tab, v_tab_t, xnt, r2, e2, n1, cf)


def _final_kernel(x1_ref, pt_ref, g_ref, o_ref):
    o_ref[...] = _rms(x1_ref[...] + pt_ref[...].T, g_ref[...])


def _final(x1, peer_t, g, tm):
    n, d = x1.shape
    return pl.pallas_call(
        _final_kernel,
        grid=(n // tm,),
        in_specs=[
            pl.BlockSpec((tm, d), lambda i: (i, 0)),
            pl.BlockSpec((d, tm), lambda i: (0, i)),
            pl.BlockSpec((1, d), lambda i: (0, 0)),
        ],
        out_specs=pl.BlockSpec((tm, d), lambda i: (i, 0)),
        out_shape=jax.ShapeDtypeStruct((n, d), F32),
        compiler_params=_params("parallel"),
        name="final",
    )(x1, peer_t, g)


def _rope_tables(seq):
    half = ROPE_DIMS // 2
    dim = jnp.arange(LANES) % QK_DIM
    inv_freq = ROPE_THETA ** (-(2 * (dim % half)).astype(F32) / ROPE_DIMS)
    ang = jnp.arange(seq, dtype=F32)[:, None] * inv_freq[None, :]
    cos, sin = jnp.cos(ang), jnp.sin(ang)
    c = jnp.where(dim < ROPE_DIMS, cos, 1.0)
    sa = jnp.where(dim < half, -sin, 0.0)
    sb = jnp.where((dim >= half) & (dim < ROPE_DIMS), sin, 0.0)
    return c, sa, sb


def _tile(n, want):
    t = min(n, want)
    assert n % t == 0, (n, t)
    return t


def kernel(x, mix_norm_g, w_in, lambda_q1, lambda_k1, lambda_q2, lambda_k2, subln_g,
           gmlp_z_norm_g, gmlp_w_s, gmlp_b_s, gmlp_out_g, w_out, ffn_norm_g,
           peer_w_q, peer_sub_keys, peer_u, peer_v, final_norm_g):
    b, s, d = x.shape
    n = b * s
    depth = w_in.shape[0]
    cos_t, sa_t, sb_t = _rope_tables(s)
    tm_in = _tile(s, 512)
    tk = _tile(s, ATTN_KEY_TILE)
    tq = _tile(s, 4 * tk)
    tm = _tile(n, 512)
    tn = _tile(n, 512)
    ec = 1024

    x2 = x.reshape(n, d)
    for l in range(depth):
        lambda_init = 0.8 - 0.6 * math.exp(-0.3 * l)
        lam = (jnp.exp(jnp.sum(lambda_q1[l] * lambda_k1[l]))
               - jnp.exp(jnp.sum(lambda_q2[l] * lambda_k2[l])) + lambda_init).reshape(1).astype(F32)

        proj = _inproj(x2, mix_norm_g[l][None], w_in[l].astype(MXU_DTYPE), cos_t, sa_t, sb_t,
                       gmlp_z_norm_g[l][None], s, tm_in)
        attn = _attn(lam, proj, subln_g[l][None], 1.0 - lambda_init, b, tq, tk)
        b_full = jnp.broadcast_to(gmlp_b_s[l][:, :, None], (GROUPS, GMLP_LEN, GROUP_CH))
        gm = _gmlp(proj, gmlp_w_s[l], b_full, gmlp_out_g[l][None], tm)
        x1, xnt = _outproj(attn, gm, x2, w_out[l].astype(MXU_DTYPE), ffn_norm_g[l][None], tm)

        r2, e2, n1, cf = _peersel(peer_w_q[l].T.astype(MXU_DTYPE),
                                  peer_sub_keys[l].astype(MXU_DTYPE), xnt, tn)
        peer_t = _peerffn(peer_u[l].astype(MXU_DTYPE), peer_v[l].T.astype(MXU_DTYPE), xnt,
                          r2, e2, n1, cf, tn, ec)
        if l + 1 < depth:
            x2 = x1 + peer_t.T
    return _final(x1, peer_t, final_norm_g[None], tm).reshape(b, s, d)
```

```python
import functools
import math

import jax
import jax.numpy as jnp
from jax import lax
from jax.experimental import pallas as pl
from jax.experimental.pallas import tpu as pltpu

F32 = jnp.float32
MXU_DTYPE = jnp.bfloat16
GATE_DTYPE = jnp.bfloat16

EPS = 1e-6
NEG_INF = -1e30
CHUNK = 64
ROPE_THETA = 500000.0

HEADS = 8
QK_DIM = 64
V_DIM = 128
ROPE_DIMS = QK_DIM // 4
GROUPS = 8
GROUP_CH = 128
GMLP_LEN = 128
WIDTH = 1024

PEER_HEADS = 8
PEER_KEYS = 128
PEER_TOPK = 16

LANES = 128
SUBLANES = 8
MXU_COLS = 256
ATTN_KEY_TILE = 2 * MXU_COLS
Q_SCALE = QK_DIM ** -0.5 * math.log2(math.e)
PEER_SUB = 2 * MXU_COLS
PEER_ROWS = 512
PEER_GATE = MXU_COLS
VMEM_LIMIT = 60 * 1024 * 1024


def _params(*sem):
    return pltpu.CompilerParams(dimension_semantics=sem, vmem_limit_bytes=VMEM_LIMIT)


def _rms(x, g):
    return x * lax.rsqrt(jnp.mean(x * x, axis=-1, keepdims=True) + EPS) * g


def _inproj_kernel(x_ref, g_ref, w_ref, cos_ref, sa_ref, sb_ref, zg_ref, o_ref):
    hn = _rms(x_ref[...], g_ref[...]).astype(w_ref.dtype)
    c, sa, sb = cos_ref[...], sa_ref[...], sb_ref[...]
    zg = zg_ref[...]
    for j in (0, 1, 3, 4, 2):
        acc = jnp.dot(hn, w_ref[:, j * WIDTH:(j + 1) * WIDTH], preferred_element_type=F32)
        for h in range(HEADS):
            t = acc[:, h * LANES:(h + 1) * LANES]
            if j < 2:
                t = t * c + pltpu.roll(t, LANES - 8, 1) * sa + pltpu.roll(t, 8, 1) * sb
                if j == 0:
                    t = t * Q_SCALE
            elif j >= 3:
                t = jax.nn.gelu(t)
                if j == 4:
                    t = _rms(t, zg[:, h * LANES:(h + 1) * LANES])
            o_ref[j * HEADS + h] = t.astype(o_ref.dtype)


def _inproj(x2, g, w, cos_t, sa_t, sb_t, zg, seq, tm):
    n, d = x2.shape
    nseq = seq // tm
    tab = pl.BlockSpec((tm, LANES), lambda i: (i % nseq, 0))
    return pl.pallas_call(
        _inproj_kernel,
        grid=(n // tm,),
        in_specs=[
            pl.BlockSpec((tm, d), lambda i: (i, 0)),
            pl.BlockSpec((1, d), lambda i: (0, 0)),
            pl.BlockSpec((d, 5 * WIDTH), lambda i: (0, 0), pipeline_mode=pl.Buffered(1)),
            tab, tab, tab,
            pl.BlockSpec((1, WIDTH), lambda i: (0, 0)),
        ],
        out_specs=pl.BlockSpec((5 * HEADS, tm, LANES), lambda i: (0, i, 0)),
        out_shape=jax.ShapeDtypeStruct((5 * HEADS, n, LANES), MXU_DTYPE),
        compiler_params=_params("parallel"),
        name="inproj",
    )(x2, g, w, cos_t, sa_t, sb_t, zg)


def _attn_kernel(lam_ref, q_ref, k_ref, v_ref, g_ref, o_ref, vt_ref, qpad_ref, s_ref, mx_ref, m_ref,
                 l_ref, acc_ref, *, tq, tk, post_scale):
    qi = pl.program_id(2)
    nkb = vt_ref.shape[0]
    nsub = tq // tk
    chains = range(2 * nsub)

    @pl.when(qi == 0)
    def _():
        def tr(c, carry):
            off = pl.multiple_of(c * tk, tk)
            vt_ref[c] = v_ref[pl.ds(off, tk), :].astype(F32).T.astype(vt_ref.dtype)
            return carry
        lax.fori_loop(0, nkb, tr, 0)

    qt = q_ref[...].astype(F32).T
    row = lax.broadcasted_iota(jnp.int32, qt.shape, 0)
    qpad_ref[...] = jnp.concatenate([jnp.where(row < QK_DIM, qt, 0.0),
                                     jnp.where(row >= QK_DIM, qt, 0.0)], axis=1).astype(qpad_ref.dtype)

    m_ref[...] = jnp.full(m_ref.shape, NEG_INF, F32)
    l_ref[...] = jnp.zeros(l_ref.shape, F32)
    acc_ref[...] = jnp.zeros(acc_ref.shape, F32)

    def cols(c):
        return slice(c * tk, (c + 1) * tk)

    def scores(kb, c):
        return jnp.dot(kb, qpad_ref[:, cols(c)], preferred_element_type=F32)

    def put(c, s):
        s_ref[:, cols(c)] = s
        mx_ref[:, cols(c)] = jnp.max(s, axis=0, keepdims=True)

    def consume(s, mx, vb, c, masked):
        cs = cols(c)
        if masked:
            kc = lax.broadcasted_iota(jnp.int32, s.shape, 0) // CHUNK
            qc = lax.broadcasted_iota(jnp.int32, s.shape, 1) // CHUNK
            s = jnp.where(kc <= qc, s, NEG_INF)
            mx = jnp.max(s, axis=0, keepdims=True)
        m_old = m_ref[:, cs]
        m_new = jnp.maximum(m_old, mx)
        alpha = jnp.exp2(m_old - m_new)
        p = jnp.exp2(s - m_new)
        l_ref[:, cs] = alpha * l_ref[:, cs] + jnp.sum(p, axis=0, keepdims=True)
        acc_ref[:, cs] = alpha * acc_ref[:, cs] + jnp.dot(
            vb, p.astype(vb.dtype), preferred_element_type=F32)
        m_ref[:, cs] = m_new

    kb0 = k_ref[pl.ds(0, tk), :]
    for c in chains:
        put(c, scores(kb0, c))

    def body(j, carry):
        off = pl.multiple_of((j + 1) * tk, tk)
        kb = k_ref[pl.ds(off, tk), :]
        vb = vt_ref[j]
        for c in chains:
            s_cur, mx_cur = s_ref[:, cols(c)], mx_ref[:, cols(c)]
            s_next = scores(kb, c)
            consume(s_cur, mx_cur, vb, c, False)
            put(c, s_next)
        return carry
    nfull = qi * nsub
    lax.fori_loop(0, nfull, body, 0)

    for d in range(nsub):
        vb = vt_ref[nfull + d]
        if d + 1 < nsub:
            off = pl.multiple_of((nfull + d + 1) * tk, tk)
            kb = k_ref[pl.ds(off, tk), :]
        for c in chains:
            u = c % nsub
            if u < d:
                continue
            s_cur, mx_cur = s_ref[:, cols(c)], mx_ref[:, cols(c)]
            if u > d:
                s_next = scores(kb, c)
            consume(s_cur, mx_cur, vb, c, u == d)
            if u > d:
                put(c, s_next)

    inv = 1.0 / l_ref[...]
    o = acc_ref[...] * inv
    ot = o[:, :tq] - lam_ref[0] * o[:, tq:]
    o_ref[...] = (_rms(ot.T, g_ref[...]) * post_scale).astype(o_ref.dtype)


def _attn(lam, proj, subln_g, post_scale, batch, tq, tk):
    n = proj.shape[1]
    s = n // batch
    nq = s // tq
    kern = functools.partial(_attn_kernel, tq=tq, tk=tk, post_scale=post_scale)
    return pl.pallas_call(
        kern,
        grid=(batch, HEADS, nq),
        in_specs=[
            pl.BlockSpec(memory_space=pltpu.SMEM),
            pl.BlockSpec((None, tq, LANES), lambda bi, h, qi: (h, bi * nq + qi, 0)),
            pl.BlockSpec((None, s, LANES), lambda bi, h, qi: (HEADS + h, bi, 0)),
            pl.BlockSpec((None, s, LANES), lambda bi, h, qi: (2 * HEADS + h, bi, 0)),
            pl.BlockSpec((1, V_DIM), lambda bi, h, qi: (0, 0)),
        ],
        out_specs=pl.BlockSpec((tq, LANES), lambda bi, h, qi: (bi * nq + qi, h)),
        out_shape=jax.ShapeDtypeStruct((n, WIDTH), MXU_DTYPE),
        scratch_shapes=[
            pltpu.VMEM((s // tk, V_DIM, tk), MXU_DTYPE),
            pltpu.VMEM((2 * QK_DIM, 2 * tq), MXU_DTYPE),
            pltpu.VMEM((tk, 2 * tq), F32),
            pltpu.VMEM((1, 2 * tq), F32),
            pltpu.VMEM((1, 2 * tq), F32),
            pltpu.VMEM((1, 2 * tq), F32),
            pltpu.VMEM((V_DIM, 2 * tq), F32),
        ],
        compiler_params=_params("parallel", "parallel", "arbitrary"),
        name="attn",
    )(lam, proj, proj, proj, subln_g)


def _gmlp_kernel(u_ref, z_ref, w_ref, b_ref, g_ref, o_ref, gm_ref):
    tm = u_ref.shape[1]
    pc = lax.broadcasted_iota(jnp.int32, (GMLP_LEN, GMLP_LEN), 0) // CHUNK
    qc = lax.broadcasted_iota(jnp.int32, (GMLP_LEN, GMLP_LEN), 1) // CHUNK
    for g in range(GROUPS):
        w = jnp.where(pc >= qc, w_ref[g], 0.0).astype(z_ref.dtype)
        bias = b_ref[g]
        cs = slice(g * GROUP_CH, (g + 1) * GROUP_CH)
        for nb in range(tm // GMLP_LEN):
            rs = slice(nb * GMLP_LEN, (nb + 1) * GMLP_LEN)
            sz = jnp.dot(w, z_ref[g, rs, :], preferred_element_type=F32) + bias
            gm_ref[rs, cs] = u_ref[g, rs, :].astype(F32) * sz
    o_ref[...] = _rms(gm_ref[...], g_ref[...]).astype(o_ref.dtype)


def _gmlp(proj, w_s, b_full, out_g, tm):
    n = proj.shape[1]
    return pl.pallas_call(
        _gmlp_kernel,
        grid=(n // tm,),
        in_specs=[
            pl.BlockSpec((GROUPS, tm, GROUP_CH), lambda i: (3, i, 0)),
            pl.BlockSpec((GROUPS, tm, GROUP_CH), lambda i: (4, i, 0)),
            pl.BlockSpec((GROUPS, GMLP_LEN, GMLP_LEN), lambda i: (0, 0, 0)),
            pl.BlockSpec((GROUPS, GMLP_LEN, GROUP_CH), lambda i: (0, 0, 0)),
            pl.BlockSpec((1, WIDTH), lambda i: (0, 0)),
        ],
        out_specs=pl.BlockSpec((tm, WIDTH), lambda i: (i, 0)),
        out_shape=jax.ShapeDtypeStruct((n, WIDTH), MXU_DTYPE),
        scratch_shapes=[pltpu.VMEM((tm, WIDTH), F32)],
        compiler_params=_params("parallel"),
        name="gmlp",
    )(proj, proj, w_s, b_full, out_g)


def _outproj_kernel(a_ref, gm_ref, x_ref, w_ref, g_ref, x1_ref, xnt_ref):
    mixed = (jnp.dot(a_ref[...], w_ref[:WIDTH, :], preferred_element_type=F32)
             + jnp.dot(gm_ref[...], w_ref[WIDTH:, :], preferred_element_type=F32))
    x1 = x_ref[...] + mixed
    x1_ref[...] = x1
    xnt_ref[...] = _rms(x1, g_ref[...]).T.astype(xnt_ref.dtype)


def _outproj(attn2, gm, x2, w_out, ffn_g, tm):
    n, d = x2.shape
    return pl.pallas_call(
        _outproj_kernel,
        grid=(n // tm,),
        in_specs=[
            pl.BlockSpec((tm, WIDTH), lambda i: (i, 0)),
            pl.BlockSpec((tm, WIDTH), lambda i: (i, 0)),
            pl.BlockSpec((tm, d), lambda i: (i, 0)),
            pl.BlockSpec((2 * WIDTH, d), lambda i: (0, 0)),
            pl.BlockSpec((1, d), lambda i: (0, 0)),
        ],
        out_specs=[
            pl.BlockSpec((tm, d), lambda i: (i, 0)),
            pl.BlockSpec((d, tm), lambda i: (0, i)),
        ],
        out_shape=[
            jax.ShapeDtypeStruct((n, d), F32),
            jax.ShapeDtypeStruct((d, n), MXU_DTYPE),
        ],
        compiler_params=_params("parallel"),
        name="outproj",
    )(attn2, gm, x2, w_out, ffn_g)


def _sort_desc(v):
    v, n, k = list(v), len(v), 2
    while k <= n:
        j = k // 2
        while j >= 1:
            for i in range(n):
                p = i ^ j
                if p > i:
                    hi, lo = jnp.maximum(v[i], v[p]), jnp.minimum(v[i], v[p])
                    v[i], v[p] = (hi, lo) if (i & k) == 0 else (lo, hi)
            j //= 2
        k *= 2
    return v


def _merge_top(lists, singles, count):
    lists, singles, rows = list(lists), list(singles), []
    for r in range(count):
        head = lists[0]
        for s in singles:
            head = jnp.maximum(head, s) if s.shape == head.shape else head
        m = jnp.max(head, axis=0, keepdims=True)
        for s in singles:
            if s.shape != head.shape:
                m = jnp.maximum(m, s)
        rows.append(m)
        if r + 1 == count:
            break
        hit = lists[0] == m
        for k in range(min(len(lists), count - r - 1)):
            nxt = lists[k + 1] if k + 1 < len(lists) else -jnp.inf
            lists[k] = jnp.where(hit, nxt, lists[k])
        singles = [jnp.where(s == m, -jnp.inf, s) for s in singles]
    return rows


def _vregs(x):
    return [x[k * SUBLANES:(k + 1) * SUBLANES, :] for k in range(x.shape[0] // SUBLANES)]


def _peersel_kernel(w_ref, k_ref, xnt_ref, r2_ref, e2_ref, n1_ref, cf_ref):
    nk = PEER_KEYS

    def scores(h):
        q = jnp.dot(w_ref[h * 2 * nk:(h + 1) * 2 * nk, :], xnt_ref[...],
                    preferred_element_type=F32).astype(xnt_ref.dtype)
        return (jnp.dot(k_ref[h, 0], q[:nk], preferred_element_type=F32),
                jnp.dot(k_ref[h, 1], q[nk:], preferred_element_type=F32))

    s = scores(0)
    for h in range(PEER_HEADS):
        if h + 1 < PEER_HEADS:
            s_next = scores(h + 1)
        rows = slice(h * nk, (h + 1) * nk)
        for c in range(xnt_ref.shape[1] // LANES):
            cs = slice(c * LANES, (c + 1) * LANES)
            r2, e2, n1, cf = _peersel_column(s[0][:, cs], s[1][:, cs])
            r2_ref[rows, cs] = r2.astype(r2_ref.dtype)
            e2_ref[rows, cs] = e2.astype(e2_ref.dtype)
            n1_ref[rows, cs] = n1
            cf_ref[rows, cs] = cf
        if h + 1 < PEER_HEADS:
            s = s_next


def _peersel_column(s1, s2):
    top = PEER_TOPK + 1
    a1 = _merge_top(_sort_desc(_vregs(s1)), [], top)
    a2 = _merge_top(_sort_desc(_vregs(s2)), [], top)

    a1lo = jnp.concatenate(a1[:SUBLANES], axis=0)
    sub = lax.broadcasted_iota(jnp.int32, a1lo.shape, 0)
    lists = [jnp.where(sub < min(SUBLANES, top // (c + 1)), a1lo + a2[c], -jnp.inf)
             for c in range(top)]
    singles = [jnp.concatenate(a1[SUBLANES:2 * SUBLANES], axis=0) + a2[0], a1[PEER_TOPK] + a2[0]]
    best = _merge_top(lists, singles, top)

    zsum = jnp.zeros_like(best[0])
    for kk in range(PEER_TOPK):
        zsum = zsum + jnp.exp(best[kk] - best[0])
    tau = 0.5 * (best[PEER_TOPK - 1] + best[PEER_TOPK])
    rank2 = jnp.full(s2.shape, PEER_KEYS - 1, F32)
    for c in reversed(range(top)):
        rank2 = jnp.where(s2 >= a2[c], float(c), rank2)
    thr = tau - s1
    n1 = jnp.zeros_like(s1)
    for c in range(SUBLANES):
        n1 = jnp.where(a2[c] >= thr, float(c + 1), n1)
    thr_top = tau - a1[0]
    n_top = jnp.zeros_like(thr_top)
    for c in range(SUBLANES, top):
        n_top = jnp.where(a2[c] >= thr_top, float(c + 1), n_top)
    n1 = jnp.where(s1 == a1[0], jnp.maximum(n1, n_top), n1)
    return rank2, jnp.exp(s2 - a2[0]), n1, jnp.exp(s1 - a1[0]) * (0.5 / zsum)


def _peersel(w_qt, keys, xnt, tn):
    dq, d = w_qt.shape
    n = xnt.shape[1]
    rows = PEER_HEADS * PEER_KEYS
    out = pl.BlockSpec((rows, tn), lambda i: (0, i))
    shape = jax.ShapeDtypeStruct((rows, n), F32)
    gshape = jax.ShapeDtypeStruct((rows, n), GATE_DTYPE)
    return pl.pallas_call(
        _peersel_kernel,
        grid=(n // tn,),
        in_specs=[
            pl.BlockSpec((dq, d), lambda i: (0, 0), pipeline_mode=pl.Buffered(1)),
            pl.BlockSpec(keys.shape, lambda i: (0, 0, 0, 0), pipeline_mode=pl.Buffered(1)),
            pl.BlockSpec((d, tn), lambda i: (0, i)),
        ],
        out_specs=[out, out, out, out],
        out_shape=[gshape, gshape, shape, shape],
        compiler_params=_params("parallel"),
        name="peersel",
    )(w_qt, keys, xnt)


def _gelu2(x):
    k0 = math.sqrt(2.0 / math.pi)
    return x + x * jnp.tanh(x * (k0 + (k0 * 0.044715) * (x * x)))


def _peerffn_kernel(u_ref, vt_ref, xnt_ref, r2_ref, e2_ref, n1_ref, cf_ref, x1_ref, g_ref, o_ref,
                    acc_ref, *, last_layer):
    e = pl.program_id(1)
    ec = u_ref.shape[0]
    nsub = ec // PEER_SUB
    tn = xnt_ref.shape[1]
    gdt = r2_ref.dtype
    zero = jnp.zeros((), gdt)
    rows = SUBLANES * (4 // jnp.dtype(gdt).itemsize)
    tile = (rows, tn)
    tiled = (PEER_KEYS // rows, rows, tn)

    def coefs(step, g):
        parts = []
        for ii in range(PEER_GATE // PEER_KEYS):
            k = g * (PEER_GATE // PEER_KEYS) + ii
            coef = jnp.zeros(tiled, gdt)
            for h in range(PEER_HEADS):
                hs = slice(h * PEER_KEYS, (h + 1) * PEER_KEYS)
                base = pl.multiple_of(h * PEER_KEYS + step * (ec // PEER_KEYS)
                                      + (k // SUBLANES) * SUBLANES, SUBLANES)
                ks = slice(k % SUBLANES, k % SUBLANES + 1)
                n1 = jnp.broadcast_to(n1_ref[pl.ds(base, SUBLANES), :][ks], tile).astype(gdt)
                cf = jnp.broadcast_to(cf_ref[pl.ds(base, SUBLANES), :][ks], tile).astype(gdt)
                r2 = r2_ref[hs, :].reshape(tiled)
                e2 = e2_ref[hs, :].reshape(tiled)
                coef = coef + jnp.where(r2 < n1[None], e2, zero) * cf[None]
            parts.append(coef.reshape(PEER_KEYS, tn))
        return jnp.concatenate(parts, axis=0).astype(F32)

    def gated(sub):
        parts = []
        for g in range(sub * PEER_SUB // PEER_GATE, (sub + 1) * PEER_SUB // PEER_GATE):
            act = jnp.dot(u_ref[g * PEER_GATE:(g + 1) * PEER_GATE, :], xnt_ref[...],
                          preferred_element_type=F32)
            parts.append((coefs(e, g) * _gelu2(act)).astype(vt_ref.dtype))
        return jnp.concatenate(parts, axis=0)

    @pl.when(e == 0)
    def _():
        acc_ref[...] = jnp.zeros(acc_ref.shape, F32)

    pt = gated(0)
    for sub in range(nsub):
        if sub + 1 < nsub:
            pt_next = gated(sub + 1)
        for r in range(0, acc_ref.shape[0], PEER_ROWS):
            acc_ref[r:r + PEER_ROWS, :] += jnp.dot(
                vt_ref[r:r + PEER_ROWS, sub * PEER_SUB:(sub + 1) * PEER_SUB], pt,
                preferred_element_type=F32)
        if sub + 1 < nsub:
            pt = pt_next

    @pl.when(e == pl.num_programs(1) - 1)
    def _():
        x2 = x1_ref[...] + acc_ref[...].T
        o_ref[...] = _rms(x2, g_ref[...]) if last_layer else x2


def _peerffn(u_tab, v_tab_t, xnt, r2, e2, n1, cf, x1, final_g, tn, ec, last_layer):
    ne, d = u_tab.shape
    n = xnt.shape[1]
    rows = PEER_HEADS * PEER_KEYS
    assert ec % (SUBLANES * PEER_KEYS) == 0 and ec % PEER_SUB == 0, ec
    sel = pl.BlockSpec((rows, tn), lambda i, e: (0, i))
    return pl.pallas_call(
        functools.partial(_peerffn_kernel, last_layer=last_layer),
        grid=(n // tn, ne // ec),
        in_specs=[
            pl.BlockSpec((ec, d), lambda i, e: (e, 0)),
            pl.BlockSpec((d, ec), lambda i, e: (0, e)),
            pl.BlockSpec((d, tn), lambda i, e: (0, i)),
            sel, sel, sel, sel,
            pl.BlockSpec((tn, d), lambda i, e: (i, 0)),
            pl.BlockSpec((1, d), lambda i, e: (0, 0)),
        ],
        out_specs=pl.BlockSpec((tn, d), lambda i, e: (i, 0)),
        out_shape=jax.ShapeDtypeStruct((n, d), F32),
        scratch_shapes=[pltpu.VMEM((d, tn), F32)],
        compiler_params=_params("parallel", "arbitrary"),
        name="peerffn",
    )(u_tab, v_tab_t, xnt, r2, e2, n1, cf, x1, final_g)


def _rope_tables(seq):
    half = ROPE_DIMS // 2
    dim = jnp.arange(LANES) % QK_DIM
    inv_freq = ROPE_THETA ** (-(2 * (dim % half)).astype(F32) / ROPE_DIMS)
    ang = jnp.arange(seq, dtype=F32)[:, None] * inv_freq[None, :]
    cos, sin = jnp.cos(ang), jnp.sin(ang)
    c = jnp.where(dim < ROPE_DIMS, cos, 1.0)
    sa = jnp.where(dim < half, -sin, 0.0)
    sb = jnp.where((dim >= half) & (dim < ROPE_DIMS), sin, 0.0)
    return c, sa, sb


def _tile(n, want):
    t = min(n, want)
    assert n % t == 0, (n, t)
    return t


def kernel(x, mix_norm_g, w_in, lambda_q1, lambda_k1, lambda_q2, lambda_k2, subln_g,
           gmlp_z_norm_g, gmlp_w_s, gmlp_b_s, gmlp_out_g, w_out, ffn_norm_g,
           peer_w_q, peer_sub_keys, peer_u, peer_v, final_norm_g):
    b, s, d = x.shape
    n = b * s
    depth = w_in.shape[0]
    cos_t, sa_t, sb_t = _rope_tables(s)
    tm_in = _tile(s, 512)
    tk = _tile(s, ATTN_KEY_TILE)
    tq = _tile(s, 4 * tk)
    tm = _tile(n, 512)
    tn = _tile(n, 512)
    ec = 1024

    x2 = x.reshape(n, d)
    for l in range(depth):
        lambda_init = 0.8 - 0.6 * math.exp(-0.3 * l)
        lam = (jnp.exp(jnp.sum(lambda_q1[l] * lambda_k1[l]))
               - jnp.exp(jnp.sum(lambda_q2[l] * lambda_k2[l])) + lambda_init).reshape(1).astype(F32)

        proj = _inproj(x2, mix_norm_g[l][None], w_in[l].astype(MXU_DTYPE), cos_t, sa_t, sb_t,
                       gmlp_z_norm_g[l][None], s, tm_in)
        attn = _attn(lam, proj, subln_g[l][None], 1.0 - lambda_init, b, tq, tk)
        b_full = jnp.broadcast_to(gmlp_b_s[l][:, :, None], (GROUPS, GMLP_LEN, GROUP_CH))
        gm = _gmlp(proj, gmlp_w_s[l], b_full, gmlp_out_g[l][None], tm)
        x1, xnt = _outproj(attn, gm, x2, w_out[l].astype(MXU_DTYPE), ffn_norm_g[l][None], tm)

        r2, e2, n1, cf = _peersel(peer_w_q[l].astype(MXU_DTYPE).T,
                                  peer_sub_keys[l].astype(MXU_DTYPE), xnt, tn)
        x2 = _peerffn(peer_u[l].astype(MXU_DTYPE), peer_v[l].astype(MXU_DTYPE).T, xnt,
                      r2, e2, n1, cf, x1, final_norm_g[None], tn, ec, l + 1 == depth)
    return x2.reshape(b, s, d)
```

```python
import functools
import math

import jax
import jax.numpy as jnp
from jax import lax
from jax.experimental import pallas as pl
from jax.experimental.pallas import tpu as pltpu

F32 = jnp.float32
MXU_DTYPE = jnp.bfloat16
GATE_DTYPE = jnp.bfloat16

EPS = 1e-6
NEG_INF = -1e30
CHUNK = 64
ROPE_THETA = 500000.0

HEADS = 8
QK_DIM = 64
V_DIM = 128
ROPE_DIMS = QK_DIM // 4
GROUPS = 8
GROUP_CH = 128
GMLP_LEN = 128
WIDTH = 1024

PEER_HEADS = 8
PEER_KEYS = 128
PEER_TOPK = 16

LANES = 128
SUBLANES = 8
MXU_COLS = 256
ATTN_KEY_TILE = 2 * MXU_COLS
Q_SCALE = QK_DIM ** -0.5 * math.log2(math.e)
PEER_SUB = 2 * MXU_COLS
PEER_ROWS = 512
PEER_GATE = PEER_KEYS
VMEM_LIMIT = 60 * 1024 * 1024


def _params(*sem):
    return pltpu.CompilerParams(dimension_semantics=sem, vmem_limit_bytes=VMEM_LIMIT)


def _rms(x, g):
    return x * lax.rsqrt(jnp.mean(x * x, axis=-1, keepdims=True) + EPS) * g


def _inproj_kernel(x_ref, g_ref, w_ref, cos_ref, sa_ref, sb_ref, zg_ref, o_ref):
    hn = _rms(x_ref[...], g_ref[...]).astype(w_ref.dtype)
    c, sa, sb = cos_ref[...], sa_ref[...], sb_ref[...]
    zg = zg_ref[...]
    for j in (0, 1, 3, 4, 2):
        acc = jnp.dot(hn, w_ref[:, j * WIDTH:(j + 1) * WIDTH], preferred_element_type=F32)
        for h in range(HEADS):
            t = acc[:, h * LANES:(h + 1) * LANES]
            if j < 2:
                t = t * c + pltpu.roll(t, LANES - 8, 1) * sa + pltpu.roll(t, 8, 1) * sb
                if j == 0:
                    t = t * Q_SCALE
            elif j >= 3:
                t = jax.nn.gelu(t)
                if j == 4:
                    t = _rms(t, zg[:, h * LANES:(h + 1) * LANES])
            o_ref[j * HEADS + h] = t.astype(o_ref.dtype)


def _inproj(x2, g, w, cos_t, sa_t, sb_t, zg, seq, tm):
    n, d = x2.shape
    nseq = seq // tm
    tab = pl.BlockSpec((tm, LANES), lambda i: (i % nseq, 0))
    return pl.pallas_call(
        _inproj_kernel,
        grid=(n // tm,),
        in_specs=[
            pl.BlockSpec((tm, d), lambda i: (i, 0)),
            pl.BlockSpec((1, d), lambda i: (0, 0)),
            pl.BlockSpec((d, 5 * WIDTH), lambda i: (0, 0), pipeline_mode=pl.Buffered(1)),
            tab, tab, tab,
            pl.BlockSpec((1, WIDTH), lambda i: (0, 0)),
        ],
        out_specs=pl.BlockSpec((5 * HEADS, tm, LANES), lambda i: (0, i, 0)),
        out_shape=jax.ShapeDtypeStruct((5 * HEADS, n, LANES), MXU_DTYPE),
        compiler_params=_params("parallel"),
        name="inproj",
    )(x2, g, w, cos_t, sa_t, sb_t, zg)


def _attn_kernel(lam_ref, q_ref, k_ref, v_ref, g_ref, o_ref, vt_ref, qpad_ref, s_ref, mx_ref, m_ref,
                 l_ref, acc_ref, *, tq, tk, post_scale):
    qi = pl.program_id(2)
    nkb = vt_ref.shape[0]
    nsub = tq // tk
    chains = range(2 * nsub)

    @pl.when(qi == 0)
    def _():
        def tr(c, carry):
            off = pl.multiple_of(c * tk, tk)
            vt_ref[c] = v_ref[pl.ds(off, tk), :].astype(F32).T.astype(vt_ref.dtype)
            return carry
        lax.fori_loop(0, nkb, tr, 0)

    qt = q_ref[...].astype(F32).T
    row = lax.broadcasted_iota(jnp.int32, qt.shape, 0)
    qpad_ref[...] = jnp.concatenate([jnp.where(row < QK_DIM, qt, 0.0),
                                     jnp.where(row >= QK_DIM, qt, 0.0)], axis=1).astype(qpad_ref.dtype)

    m_ref[...] = jnp.full(m_ref.shape, NEG_INF, F32)
    l_ref[...] = jnp.zeros(l_ref.shape, F32)
    acc_ref[...] = jnp.zeros(acc_ref.shape, F32)

    def cols(c):
        return slice(c * tk, (c + 1) * tk)

    def scores(kb, c):
        return jnp.dot(kb, qpad_ref[:, cols(c)], preferred_element_type=F32)

    def put(c, s):
        s_ref[:, cols(c)] = s
        mx_ref[:, cols(c)] = jnp.max(s, axis=0, keepdims=True)

    def consume(s, mx, vb, c, masked):
        cs = cols(c)
        if masked:
            kc = lax.broadcasted_iota(jnp.int32, s.shape, 0) // CHUNK
            qc = lax.broadcasted_iota(jnp.int32, s.shape, 1) // CHUNK
            s = jnp.where(kc <= qc, s, NEG_INF)
            mx = jnp.max(s, axis=0, keepdims=True)
        m_old = m_ref[:, cs]
        m_new = jnp.maximum(m_old, mx)
        alpha = jnp.exp2(m_old - m_new)
        p = jnp.exp2(s - m_new)
        l_ref[:, cs] = alpha * l_ref[:, cs] + jnp.sum(p, axis=0, keepdims=True)
        acc_ref[:, cs] = alpha * acc_ref[:, cs] + jnp.dot(
            vb, p.astype(vb.dtype), preferred_element_type=F32)
        m_ref[:, cs] = m_new

    kb0 = k_ref[pl.ds(0, tk), :]
    for c in chains:
        put(c, scores(kb0, c))

    def body(j, carry):
        off = pl.multiple_of((j + 1) * tk, tk)
        kb = k_ref[pl.ds(off, tk), :]
        vb = vt_ref[j]
        for c in chains:
            s_cur, mx_cur = s_ref[:, cols(c)], mx_ref[:, cols(c)]
            s_next = scores(kb, c)
            consume(s_cur, mx_cur, vb, c, False)
            put(c, s_next)
        return carry
    nfull = qi * nsub
    lax.fori_loop(0, nfull, body, 0)

    for d in range(nsub):
        vb = vt_ref[nfull + d]
        if d + 1 < nsub:
            off = pl.multiple_of((nfull + d + 1) * tk, tk)
            kb = k_ref[pl.ds(off, tk), :]
        for c in chains:
            u = c % nsub
            if u < d:
                continue
            s_cur, mx_cur = s_ref[:, cols(c)], mx_ref[:, cols(c)]
            if u > d:
                s_next = scores(kb, c)
            consume(s_cur, mx_cur, vb, c, u == d)
            if u > d:
                put(c, s_next)

    inv = 1.0 / l_ref[...]
    o = acc_ref[...] * inv
    ot = o[:, :tq] - lam_ref[0] * o[:, tq:]
    o_ref[...] = (_rms(ot.T, g_ref[...]) * post_scale).astype(o_ref.dtype)


def _attn(lam, proj, subln_g, post_scale, batch, tq, tk):
    n = proj.shape[1]
    s = n // batch
    nq = s // tq
    kern = functools.partial(_attn_kernel, tq=tq, tk=tk, post_scale=post_scale)
    return pl.pallas_call(
        kern,
        grid=(batch, HEADS, nq),
        in_specs=[
            pl.BlockSpec(memory_space=pltpu.SMEM),
            pl.BlockSpec((None, tq, LANES), lambda bi, h, qi: (h, bi * nq + qi, 0)),
            pl.BlockSpec((None, s, LANES), lambda bi, h, qi: (HEADS + h, bi, 0)),
            pl.BlockSpec((None, s, LANES), lambda bi, h, qi: (2 * HEADS + h, bi, 0)),
            pl.BlockSpec((1, V_DIM), lambda bi, h, qi: (0, 0)),
        ],
        out_specs=pl.BlockSpec((tq, LANES), lambda bi, h, qi: (bi * nq + qi, h)),
        out_shape=jax.ShapeDtypeStruct((n, WIDTH), MXU_DTYPE),
        scratch_shapes=[
            pltpu.VMEM((s // tk, V_DIM, tk), MXU_DTYPE),
            pltpu.VMEM((2 * QK_DIM, 2 * tq), MXU_DTYPE),
            pltpu.VMEM((tk, 2 * tq), F32),
            pltpu.VMEM((1, 2 * tq), F32),
            pltpu.VMEM((1, 2 * tq), F32),
            pltpu.VMEM((1, 2 * tq), F32),
            pltpu.VMEM((V_DIM, 2 * tq), F32),
        ],
        compiler_params=_params("parallel", "parallel", "arbitrary"),
        name="attn",
    )(lam, proj, proj, proj, subln_g)


def _gmlp_kernel(u_ref, z_ref, w_ref, b_ref, g_ref, o_ref, gm_ref):
    tm = u_ref.shape[1]
    pc = lax.broadcasted_iota(jnp.int32, (GMLP_LEN, GMLP_LEN), 0) // CHUNK
    qc = lax.broadcasted_iota(jnp.int32, (GMLP_LEN, GMLP_LEN), 1) // CHUNK
    for g in range(GROUPS):
        w = jnp.where(pc >= qc, w_ref[g], 0.0).astype(z_ref.dtype)
        bias = b_ref[g]
        cs = slice(g * GROUP_CH, (g + 1) * GROUP_CH)
        for nb in range(tm // GMLP_LEN):
            rs = slice(nb * GMLP_LEN, (nb + 1) * GMLP_LEN)
            sz = jnp.dot(w, z_ref[g, rs, :], preferred_element_type=F32) + bias
            gm_ref[rs, cs] = u_ref[g, rs, :].astype(F32) * sz
    o_ref[...] = _rms(gm_ref[...], g_ref[...]).astype(o_ref.dtype)


def _gmlp(proj, w_s, b_full, out_g, tm):
    n = proj.shape[1]
    return pl.pallas_call(
        _gmlp_kernel,
        grid=(n // tm,),
        in_specs=[
            pl.BlockSpec((GROUPS, tm, GROUP_CH), lambda i: (3, i, 0)),
            pl.BlockSpec((GROUPS, tm, GROUP_CH), lambda i: (4, i, 0)),
            pl.BlockSpec((GROUPS, GMLP_LEN, GMLP_LEN), lambda i: (0, 0, 0)),
            pl.BlockSpec((GROUPS, GMLP_LEN, GROUP_CH), lambda i: (0, 0, 0)),
            pl.BlockSpec((1, WIDTH), lambda i: (0, 0)),
        ],
        out_specs=pl.BlockSpec((tm, WIDTH), lambda i: (i, 0)),
        out_shape=jax.ShapeDtypeStruct((n, WIDTH), MXU_DTYPE),
        scratch_shapes=[pltpu.VMEM((tm, WIDTH), F32)],
        compiler_params=_params("parallel"),
        name="gmlp",
    )(proj, proj, w_s, b_full, out_g)


def _outproj_kernel(a_ref, gm_ref, x_ref, w_ref, g_ref, x1_ref, xnt_ref):
    mixed = (jnp.dot(a_ref[...], w_ref[:WIDTH, :], preferred_element_type=F32)
             + jnp.dot(gm_ref[...], w_ref[WIDTH:, :], preferred_element_type=F32))
    x1 = x_ref[...] + mixed
    x1_ref[...] = x1
    xnt_ref[...] = _rms(x1, g_ref[...]).T.astype(xnt_ref.dtype)


def _outproj(attn2, gm, x2, w_out, ffn_g, tm):
    n, d = x2.shape
    return pl.pallas_call(
        _outproj_kernel,
        grid=(n // tm,),
        in_specs=[
            pl.BlockSpec((tm, WIDTH), lambda i: (i, 0)),
            pl.BlockSpec((tm, WIDTH), lambda i: (i, 0)),
            pl.BlockSpec((tm, d), lambda i: (i, 0)),
            pl.BlockSpec((2 * WIDTH, d), lambda i: (0, 0)),
            pl.BlockSpec((1, d), lambda i: (0, 0)),
        ],
        out_specs=[
            pl.BlockSpec((tm, d), lambda i: (i, 0)),
            pl.BlockSpec((d, tm), lambda i: (0, i)),
        ],
        out_shape=[
            jax.ShapeDtypeStruct((n, d), F32),
            jax.ShapeDtypeStruct((d, n), MXU_DTYPE),
        ],
        compiler_params=_params("parallel"),
        name="outproj",
    )(attn2, gm, x2, w_out, ffn_g)


def _sort_desc(v):
    v, n, k = list(v), len(v), 2
    while k <= n:
        j = k // 2
        while j >= 1:
            for i in range(n):
                p = i ^ j
                if p > i:
                    hi, lo = jnp.maximum(v[i], v[p]), jnp.minimum(v[i], v[p])
                    v[i], v[p] = (hi, lo) if (i & k) == 0 else (lo, hi)
            j //= 2
        k *= 2
    return v


def _merge_top(lists, singles, count):
    lists, singles, rows = list(lists), list(singles), []
    for r in range(count):
        head = lists[0]
        for s in singles:
            head = jnp.maximum(head, s) if s.shape == head.shape else head
        m = jnp.max(head, axis=0, keepdims=True)
        for s in singles:
            if s.shape != head.shape:
                m = jnp.maximum(m, s)
        rows.append(m)
        if r + 1 == count:
            break
        hit = lists[0] == m
        for k in range(min(len(lists), count - r - 1)):
            nxt = lists[k + 1] if k + 1 < len(lists) else -jnp.inf
            lists[k] = jnp.where(hit, nxt, lists[k])
        singles = [jnp.where(s == m, -jnp.inf, s) for s in singles]
    return rows


def _vregs(x):
    return [x[k * SUBLANES:(k + 1) * SUBLANES, :] for k in range(x.shape[0] // SUBLANES)]


def _peersel_kernel(w_ref, k_ref, xnt_ref, r2_ref, e2_ref, n1_ref, cf_ref):
    nk = PEER_KEYS

    def scores(h):
        q = jnp.dot(w_ref[h * 2 * nk:(h + 1) * 2 * nk, :], xnt_ref[...],
                    preferred_element_type=F32).astype(xnt_ref.dtype)
        return (jnp.dot(k_ref[h, 0], q[:nk], preferred_element_type=F32),
                jnp.dot(k_ref[h, 1], q[nk:], preferred_element_type=F32))

    s = scores(0)
    for h in range(PEER_HEADS):
        if h + 1 < PEER_HEADS:
            s_next = scores(h + 1)
        rows = slice(h * nk, (h + 1) * nk)
        for c in range(xnt_ref.shape[1] // LANES):
            cs = slice(c * LANES, (c + 1) * LANES)
            r2, e2, n1, cf = _peersel_column(s[0][:, cs], s[1][:, cs])
            r2_ref[rows, cs] = r2.astype(r2_ref.dtype)
            e2_ref[rows, cs] = e2.astype(e2_ref.dtype)
            n1_ref[rows, cs] = n1
            cf_ref[rows, cs] = cf
        if h + 1 < PEER_HEADS:
            s = s_next


def _peersel_column(s1, s2):
    top = PEER_TOPK + 1
    a1 = _merge_top(_sort_desc(_vregs(s1)), [], top)
    a2 = _merge_top(_sort_desc(_vregs(s2)), [], top)

    a1lo = jnp.concatenate(a1[:SUBLANES], axis=0)
    sub = lax.broadcasted_iota(jnp.int32, a1lo.shape, 0)
    lists = [jnp.where(sub < min(SUBLANES, top // (c + 1)), a1lo + a2[c], -jnp.inf)
             for c in range(top)]
    singles = [jnp.concatenate(a1[SUBLANES:2 * SUBLANES], axis=0) + a2[0], a1[PEER_TOPK] + a2[0]]
    best = _merge_top(lists, singles, top)

    zsum = jnp.zeros_like(best[0])
    for kk in range(PEER_TOPK):
        zsum = zsum + jnp.exp(best[kk] - best[0])
    tau = 0.5 * (best[PEER_TOPK - 1] + best[PEER_TOPK])
    rank2 = jnp.full(s2.shape, PEER_KEYS - 1, F32)
    for c in reversed(range(top)):
        rank2 = jnp.where(s2 >= a2[c], float(c), rank2)
    thr = tau - s1
    n1 = jnp.zeros_like(s1)
    for c in range(SUBLANES):
        n1 = jnp.where(a2[c] >= thr, float(c + 1), n1)
    thr_top = tau - a1[0]
    n_top = jnp.zeros_like(thr_top)
    for c in range(SUBLANES, top):
        n_top = jnp.where(a2[c] >= thr_top, float(c + 1), n_top)
    n1 = jnp.where(s1 == a1[0], jnp.maximum(n1, n_top), n1)
    return rank2, jnp.exp(s2 - a2[0]), n1, jnp.exp(s1 - a1[0]) * (0.5 / zsum)


def _peersel(w_qt, keys, xnt, tn):
    dq, d = w_qt.shape
    n = xnt.shape[1]
    rows = PEER_HEADS * PEER_KEYS
    out = pl.BlockSpec((rows, tn), lambda i: (0, i))
    shape = jax.ShapeDtypeStruct((rows, n), F32)
    gshape = jax.ShapeDtypeStruct((rows, n), GATE_DTYPE)
    return pl.pallas_call(
        _peersel_kernel,
        grid=(n // tn,),
        in_specs=[
            pl.BlockSpec((dq, d), lambda i: (0, 0), pipeline_mode=pl.Buffered(1)),
            pl.BlockSpec(keys.shape, lambda i: (0, 0, 0, 0), pipeline_mode=pl.Buffered(1)),
            pl.BlockSpec((d, tn), lambda i: (0, i)),
        ],
        out_specs=[out, out, out, out],
        out_shape=[gshape, gshape, shape, shape],
        compiler_params=_params("parallel"),
        name="peersel",
    )(w_qt, keys, xnt)


def _gelu2(x):
    k0 = math.sqrt(2.0 / math.pi)
    return x + x * jnp.tanh(x * (k0 + (k0 * 0.044715) * (x * x)))


def _peerffn_kernel(u_ref, vt_ref, xnt_ref, r2_ref, e2_ref, n1_ref, cf_ref, x1_ref, g_ref, o_ref,
                    acc_ref, *, last_layer):
    e = pl.program_id(1)
    ec = u_ref.shape[0]
    nsub = ec // PEER_SUB
    tn = xnt_ref.shape[1]
    gdt = r2_ref.dtype
    zero = jnp.zeros((), gdt)
    rows = SUBLANES * (4 // jnp.dtype(gdt).itemsize)
    tile = (rows, tn)
    tiled = (PEER_KEYS // rows, rows, tn)

    def coefs(step, g):
        parts = []
        for ii in range(PEER_GATE // PEER_KEYS):
            k = g * (PEER_GATE // PEER_KEYS) + ii
            coef = jnp.zeros(tiled, gdt)
            for h in range(PEER_HEADS):
                hs = slice(h * PEER_KEYS, (h + 1) * PEER_KEYS)
                base = pl.multiple_of(h * PEER_KEYS + step * (ec // PEER_KEYS)
                                      + (k // SUBLANES) * SUBLANES, SUBLANES)
                ks = slice(k % SUBLANES, k % SUBLANES + 1)
                n1 = jnp.broadcast_to(n1_ref[pl.ds(base, SUBLANES), :][ks], tile).astype(gdt)
                cf = jnp.broadcast_to(cf_ref[pl.ds(base, SUBLANES), :][ks], tile).astype(gdt)
                r2 = r2_ref[hs, :].reshape(tiled)
                e2 = e2_ref[hs, :].reshape(tiled)
                coef = coef + jnp.where(r2 < n1[None], e2, zero) * cf[None]
            parts.append(coef.reshape(PEER_KEYS, tn))
        return jnp.concatenate(parts, axis=0).astype(F32)

    def gated(sub):
        parts = []
        for g in range(sub * PEER_SUB // PEER_GATE, (sub + 1) * PEER_SUB // PEER_GATE):
            act = jnp.dot(u_ref[g * PEER_GATE:(g + 1) * PEER_GATE, :], xnt_ref[...],
                          preferred_element_type=F32)
            parts.append((coefs(e, g) * _gelu2(act)).astype(vt_ref.dtype))
        return jnp.concatenate(parts, axis=0)

    @pl.when(e == 0)
    def _():
        acc_ref[...] = jnp.zeros(acc_ref.shape, F32)

    pt = gated(0)
    for sub in range(nsub):
        if sub + 1 < nsub:
            pt_next = gated(sub + 1)
        for r in range(0, acc_ref.shape[0], PEER_ROWS):
            acc_ref[r:r + PEER_ROWS, :] += jnp.dot(
                vt_ref[r:r + PEER_ROWS, sub * PEER_SUB:(sub + 1) * PEER_SUB], pt,
                preferred_element_type=F32)
        if sub + 1 < nsub:
            pt = pt_next

    @pl.when(e == pl.num_programs(1) - 1)
    def _():
        x2 = x1_ref[...] + acc_ref[...].T
        o_ref[...] = _rms(x2, g_ref[...]) if last_layer else x2


def _peerffn(u_tab, v_tab_t, xnt, r2, e2, n1, cf, x1, final_g, tn, ec, last_layer):
    ne, d = u_tab.shape
    n = xnt.shape[1]
    rows = PEER_HEADS * PEER_KEYS
    assert ec % (SUBLANES * PEER_KEYS) == 0 and ec % PEER_SUB == 0, ec
    sel = pl.BlockSpec((rows, tn), lambda i, e: (0, i))
    return pl.pallas_call(
        functools.partial(_peerffn_kernel, last_layer=last_layer),
        grid=(n // tn, ne // ec),
        in_specs=[
            pl.BlockSpec((ec, d), lambda i, e: (e, 0)),
            pl.BlockSpec((d, ec), lambda i, e: (0, e)),
            pl.BlockSpec((d, tn), lambda i, e: (0, i)),
            sel, sel, sel, sel,
            pl.BlockSpec((tn, d), lambda i, e: (i, 0)),
            pl.BlockSpec((1, d), lambda i, e: (0, 0)),
        ],
        out_specs=pl.BlockSpec((tn, d), lambda i, e: (i, 0)),
        out_shape=jax.ShapeDtypeStruct((n, d), F32),
        scratch_shapes=[pltpu.VMEM((d, tn), F32)],
        compiler_params=_params("parallel", "arbitrary"),
        name="peerffn",
    )(u_tab, v_tab_t, xnt, r2, e2, n1, cf, x1, final_g)


def _rope_tables(seq):
    half = ROPE_DIMS // 2
    dim = jnp.arange(LANES) % QK_DIM
    inv_freq = ROPE_THETA ** (-(2 * (dim % half)).astype(F32) / ROPE_DIMS)
    ang = jnp.arange(seq, dtype=F32)[:, None] * inv_freq[None, :]
    cos, sin = jnp.cos(ang), jnp.sin(ang)
    c = jnp.where(dim < ROPE_DIMS, cos, 1.0)
    sa = jnp.where(dim < half, -sin, 0.0)
    sb = jnp.where((dim >= half) & (dim < ROPE_DIMS), sin, 0.0)
    return c, sa, sb


def _tile(n, want):
    t = min(n, want)
    assert n % t == 0, (n, t)
    return t


def kernel(x, mix_norm_g, w_in, lambda_q1, lambda_k1, lambda_q2, lambda_k2, subln_g,
           gmlp_z_norm_g, gmlp_w_s, gmlp_b_s, gmlp_out_g, w_out, ffn_norm_g,
           peer_w_q, peer_sub_keys, peer_u, peer_v, final_norm_g):
    b, s, d = x.shape
    n = b * s
    depth = w_in.shape[0]
    cos_t, sa_t, sb_t = _rope_tables(s)
    tm_in = _tile(s, 512)
    tk = _tile(s, ATTN_KEY_TILE)
    tq = _tile(s, 4 * tk)
    tm = _tile(n, 512)
    tn = _tile(n, 512)
    ec = 1024

    x2 = x.reshape(n, d)
    for l in range(depth):
        lambda_init = 0.8 - 0.6 * math.exp(-0.3 * l)
        lam = (jnp.exp(jnp.sum(lambda_q1[l] * lambda_k1[l]))
               - jnp.exp(jnp.sum(lambda_q2[l] * lambda_k2[l])) + lambda_init).reshape(1).astype(F32)

        proj = _inproj(x2, mix_norm_g[l][None], w_in[l].astype(MXU_DTYPE), cos_t, sa_t, sb_t,
                       gmlp_z_norm_g[l][None], s, tm_in)
        attn = _attn(lam, proj, subln_g[l][None], 1.0 - lambda_init, b, tq, tk)
        b_full = jnp.broadcast_to(gmlp_b_s[l][:, :, None], (GROUPS, GMLP_LEN, GROUP_CH))
        gm = _gmlp(proj, gmlp_w_s[l], b_full, gmlp_out_g[l][None], tm)
        x1, xnt = _outproj(attn, gm, x2, w_out[l].astype(MXU_DTYPE), ffn_norm_g[l][None], tm)

        r2, e2, n1, cf = _peersel(peer_w_q[l].astype(MXU_DTYPE).T,
                                  peer_sub_keys[l].astype(MXU_DTYPE), xnt, tn)
        x2 = _peerffn(peer_u[l].astype(MXU_DTYPE), peer_v[l].astype(MXU_DTYPE).T, xnt,
                      r2, e2, n1, cf, x1, final_norm_g[None], tn, ec, l + 1 == depth)
    return x2.reshape(b, s, d)
```
